```python
import jax, jax.numpy as jnp
from jax import lax
import numpy as np

D_MODEL = 1024
BATCH = 8
SEQ = 4096
DEPTH = 4

N_MIXERS = 2
N_LAYERS_A = (DEPTH + 1) // 2
N_LAYERS_B = DEPTH // 2

DEEPNORM_ALPHA = (2.0 * DEPTH) ** 0.25
DEEPNORM_BETA = (8.0 * DEPTH) ** -0.25
LN_EPS = 1e-5

GLA_HEADS = 4
GLA_DK = D_MODEL // 2 // GLA_HEADS
GLA_DV = D_MODEL // GLA_HEADS
GLA_QK = GLA_HEADS * GLA_DK
GLA_VO = GLA_HEADS * GLA_DV
GLA_GATE_RANK = 16
GLA_GATE_TAU = 16.0
GLA_CHUNK = 64
GLA_IN = 2 * GLA_QK + 2 * GLA_VO + 2 * GLA_GATE_RANK
GLA_SPLITS = [GLA_QK, 2 * GLA_QK, 2 * GLA_QK + GLA_VO, 2 * GLA_QK + 2 * GLA_VO,
              2 * GLA_QK + 2 * GLA_VO + GLA_GATE_RANK]

SGU_CHUNK = 128
SGU_GROUPS = 8
SGU_WIDTH = D_MODEL
SGU_GROUP_DIM = SGU_WIDTH // SGU_GROUPS

N_EXPERTS = 32
TOP_K = 4
D_EXPERT = D_MODEL
SWIGLU_LIMIT = 7.0
SWIGLU_ALPHA = 1.702
MOE_BLOCK = 256

kernel_name = 'hybrid_gla_sgu_moe_deepnorm'


def layer_norm(x, g, b):
    xf = x.astype(jnp.float32)
    mu = jnp.mean(xf, axis=-1, keepdims=True)
    var = jnp.mean(jnp.square(xf - mu), axis=-1, keepdims=True)
    y = (xf - mu) * lax.rsqrt(var + LN_EPS)
    return (y * g.astype(jnp.float32) + b.astype(jnp.float32)).astype(x.dtype)


def gla_scan(q, k, v, logg):
    bsz, nh, s, dk = q.shape
    dv = v.shape[-1]
    nc = s // GLA_CHUNK

    def to_chunks(t):
        t = t.reshape(bsz, nh, nc, GLA_CHUNK, t.shape[-1])
        return jnp.moveaxis(t, 2, 0)

    mask = jnp.tril(jnp.ones((GLA_CHUNK, GLA_CHUNK), dtype=bool))[:, :, None]

    def step(state, inp):
        qc, kc, vc, gc = inp
        b = jnp.cumsum(gc, axis=-2)
        o_inter = jnp.einsum('bhid,bhde->bhie', qc * jnp.exp(b), state)
        diff = b[:, :, :, None, :] - b[:, :, None, :, :]
        decay = jnp.exp(jnp.where(mask, diff, -jnp.inf))
        scores = jnp.einsum('bhid,bhjd,bhijd->bhij', qc, kc, decay)
        o_intra = jnp.einsum('bhij,bhje->bhie', scores, vc)
        b_last = b[:, :, -1:, :]
        new_state = (jnp.exp(b_last[:, :, 0, :])[..., None] * state
                     + jnp.einsum('bhjd,bhje->bhde', kc * jnp.exp(b_last - b), vc))
        return new_state, o_inter + o_intra

    s0 = jnp.zeros((bsz, nh, dk, dv), jnp.float32)
    _, o = lax.scan(step, s0, (to_chunks(q), to_chunks(k), to_chunks(v), to_chunks(logg)))
    return jnp.moveaxis(o, 0, 2).reshape(bsz, nh, s, dv)


def gla_mixer(h, w_in, wg2_f, bg_f, wg2_b, bg_b, norm_g, w_out):
    bsz, s, _ = h.shape
    proj = h @ w_in
    q, k, v, r, lr_f, lr_b = jnp.split(proj, GLA_SPLITS, axis=-1)

    def heads(t, d):
        return t.reshape(bsz, s, GLA_HEADS, d).transpose(0, 2, 1, 3).astype(jnp.float32)

    qh = heads(q, GLA_DK) * (GLA_DK ** -0.5)
    kh = heads(k, GLA_DK)
    vh = heads(v, GLA_DV)
    logg_f = heads(jax.nn.log_sigmoid((lr_f @ wg2_f + bg_f).astype(jnp.float32)) / GLA_GATE_TAU, GLA_DK)
    logg_b = heads(jax.nn.log_sigmoid((lr_b @ wg2_b + bg_b).astype(jnp.float32)) / GLA_GATE_TAU, GLA_DK)

    o_f = gla_scan(qh, kh, vh, logg_f)
    flip = lambda t: jnp.flip(t, axis=2)
    o_b = flip(gla_scan(flip(qh), flip(kh), flip(vh), flip(logg_b)))
    o = o_f + o_b
    o = o * lax.rsqrt(jnp.mean(o * o, axis=-1, keepdims=True) + LN_EPS)
    o = o.astype(h.dtype) * norm_g
    o = o.transpose(0, 2, 1, 3).reshape(bsz, s, GLA_VO)
    return (o * jax.nn.silu(r)) @ w_out


def sgu_mixer(h, w_in, b_in, ln_g, ln_b, w_s, b_s, w_out, b_out):
    bsz, s, _ = h.shape
    z = jax.nn.gelu(h @ w_in + b_in, approximate=False)
    u, v = z[..., :SGU_WIDTH], z[..., SGU_WIDTH:]
    v = layer_norm(v, ln_g, ln_b)
    nch = s // SGU_CHUNK
    v = v.reshape(bsz, nch, SGU_CHUNK, SGU_GROUPS, SGU_GROUP_DIM)
    sv = jnp.einsum('gpq,bnqgc->bnpgc', w_s, v) + b_s.T[None, None, :, :, None]
    sv = sv.reshape(bsz, s, SGU_WIDTH)
    return (u * sv) @ w_out + b_out


def clamped_swiglu(hu):
    gate, lin = hu[..., ::2], hu[..., 1::2]
    gate = jnp.minimum(gate, SWIGLU_LIMIT)
    lin = jnp.clip(lin, -SWIGLU_LIMIT, SWIGLU_LIMIT)
    return gate * jax.nn.sigmoid(SWIGLU_ALPHA * gate) * (lin + 1.0)


def moe_ffn(h, w_router, b_router, w_up, b_up, w_down, b_down):
    bsz, s, d = h.shape
    n_tok = bsz * s
    xt = h.reshape(n_tok, d)
    logits = (xt @ w_router + b_router).astype(jnp.float32)
    top_vals, top_idx = lax.top_k(logits, TOP_K)
    top_w = jax.nn.softmax(top_vals, axis=-1).astype(h.dtype)

    n_assign = n_tok * TOP_K
    flat_e = top_idx.reshape(-1).astype(jnp.int32)
    flat_w = top_w.reshape(-1)
    flat_tok = jnp.arange(n_assign, dtype=jnp.int32) // TOP_K
    order = jnp.argsort(flat_e)
    se, stok, sw = flat_e[order], flat_tok[order], flat_w[order]

    counts = jnp.zeros((N_EXPERTS,), jnp.int32).at[flat_e].add(1)
    padded = (counts + MOE_BLOCK - 1) // MOE_BLOCK * MOE_BLOCK
    grp_start = jnp.cumsum(counts) - counts
    pad_end = jnp.cumsum(padded)
    pad_start = pad_end - padded
    dest = pad_start[se] + (jnp.arange(n_assign, dtype=jnp.int32) - grp_start[se])
    n_blocks = -(-(n_assign + N_EXPERTS * MOE_BLOCK) // MOE_BLOCK)
    cap = n_blocks * MOE_BLOCK
    buf_tok = jnp.zeros((cap,), jnp.int32).at[dest].set(stok)
    buf_w = jnp.zeros((cap,), h.dtype).at[dest].set(sw)
    blk_start = jnp.arange(n_blocks, dtype=jnp.int32) * MOE_BLOCK
    blk_e = jnp.minimum(jnp.searchsorted(pad_end, blk_start, side='right'), N_EXPERTS - 1).astype(jnp.int32)

    xb = xt[buf_tok].reshape(n_blocks, MOE_BLOCK, d)

    def expert_block(args):
        xblk, e = args
        hu = xblk @ w_up[e] + b_up[e]
        return clamped_swiglu(hu) @ w_down[e] + b_down[e]

    yb = lax.map(expert_block, (xb, blk_e)).reshape(cap, d) * buf_w[:, None]
    return jnp.zeros((n_tok, d), h.dtype).at[buf_tok].add(yb).reshape(bsz, s, d)


def setup_inputs(seed: int = 0) -> dict:
    key = jax.random.key(seed)
    ks = jax.random.split(key, 32)
    nrm = lambda k, shape, sc: jax.random.normal(k, shape, jnp.float32) * sc
    d = D_MODEL
    return {
        'x': nrm(ks[0], (BATCH, SEQ, d), 1.0),
        'c': nrm(ks[1], (BATCH, d), 1.0),
        'ada_w': nrm(ks[2], (DEPTH, d, 6 * d), 0.1 * d ** -0.5),
        'ada_b': nrm(ks[3], (DEPTH, 6 * d), 0.01),
        'ln_g': 1.0 + nrm(ks[4], (DEPTH, 2, d), 0.02),
        'ln_b': nrm(ks[5], (DEPTH, 2, d), 0.02),
        'gla_w_in': nrm(ks[6], (N_LAYERS_A, d, GLA_IN), d ** -0.5),
        'gla_wg2_f': nrm(ks[7], (N_LAYERS_A, GLA_GATE_RANK, GLA_QK), GLA_GATE_RANK ** -0.5),
        'gla_bg_f': nrm(ks[8], (N_LAYERS_A, GLA_QK), 0.1),
        'gla_wg2_b': nrm(ks[9], (N_LAYERS_A, GLA_GATE_RANK, GLA_QK), GLA_GATE_RANK ** -0.5),
        'gla_bg_b': nrm(ks[10], (N_LAYERS_A, GLA_QK), 0.1),
        'gla_norm_g': 1.0 + nrm(ks[11], (N_LAYERS_A, GLA_DV), 0.02),
        'gla_w_out': nrm(ks[12], (N_LAYERS_A, GLA_VO, d), DEEPNORM_BETA * GLA_VO ** -0.5),
        'sgu_w_in': nrm(ks[13], (N_LAYERS_B, d, 2 * SGU_WIDTH), d ** -0.5),
        'sgu_b_in': nrm(ks[14], (N_LAYERS_B, 2 * SGU_WIDTH), 0.02),
        'sgu_ln_g': 1.0 + nrm(ks[15], (N_LAYERS_B, SGU_WIDTH), 0.02),
        'sgu_ln_b': nrm(ks[16], (N_LAYERS_B, SGU_WIDTH), 0.02),
        'sgu_w_s': nrm(ks[17], (N_LAYERS_B, SGU_GROUPS, SGU_CHUNK, SGU_CHUNK), 0.5 * SGU_CHUNK ** -0.5),
        'sgu_b_s': 1.0 + nrm(ks[18], (N_LAYERS_B, SGU_GROUPS, SGU_CHUNK), 0.02),
        'sgu_w_out': nrm(ks[19], (N_LAYERS_B, SGU_WIDTH, d), DEEPNORM_BETA * SGU_WIDTH ** -0.5),
        'sgu_b_out': nrm(ks[20], (N_LAYERS_B, d), 0.02),
        'moe_w_router': nrm(ks[21], (DEPTH, d, N_EXPERTS), d ** -0.5),
        'moe_b_router': nrm(ks[22], (DEPTH, N_EXPERTS), 0.01),
        'moe_w_up': nrm(ks[23], (DEPTH, N_EXPERTS, d, 2 * D_EXPERT), d ** -0.5),
        'moe_b_up': nrm(ks[24], (DEPTH, N_EXPERTS, 2 * D_EXPERT), 0.02),
        'moe_w_down': nrm(ks[25], (DEPTH, N_EXPERTS, D_EXPERT, d), DEEPNORM_BETA * D_EXPERT ** -0.5),
        'moe_b_down': nrm(ks[26], (DEPTH, N_EXPERTS, d), 0.02),
    }


def reference(x, c, ada_w, ada_b, ln_g, ln_b,
              gla_w_in, gla_wg2_f, gla_bg_f, gla_wg2_b, gla_bg_b, gla_norm_g, gla_w_out,
              sgu_w_in, sgu_b_in, sgu_ln_g, sgu_ln_b, sgu_w_s, sgu_b_s, sgu_w_out, sgu_b_out,
              moe_w_router, moe_b_router, moe_w_up, moe_b_up, moe_w_down, moe_b_down):
    mod_all = jnp.einsum('bd,lde->lbe', jax.nn.silu(c), ada_w) + ada_b[:, None, :]
    for layer in range(DEPTH):
        shift1, scale1, gate1, shift2, scale2, gate2 = jnp.split(mod_all[layer], 6, axis=-1)
        h = x * (1.0 + scale1[:, None, :]) + shift1[:, None, :]
        i = layer // N_MIXERS
        if layer % N_MIXERS == 0:
            y = gla_mixer(h, gla_w_in[i], gla_wg2_f[i], gla_bg_f[i], gla_wg2_b[i], gla_bg_b[i],
                          gla_norm_g[i], gla_w_out[i])
        else:
            y = sgu_mixer(h, sgu_w_in[i], sgu_b_in[i], sgu_ln_g[i], sgu_ln_b[i], sgu_w_s[i],
                          sgu_b_s[i], sgu_w_out[i], sgu_b_out[i])
        x = layer_norm(DEEPNORM_ALPHA * x + (1.0 + gate1[:, None, :]) * y, ln_g[layer, 0], ln_b[layer, 0])
        h = x * (1.0 + scale2[:, None, :]) + shift2[:, None, :]
        y = moe_ffn(h, moe_w_router[layer], moe_b_router[layer], moe_w_up[layer], moe_b_up[layer],
                    moe_w_down[layer], moe_b_down[layer])
        x = layer_norm(DEEPNORM_ALPHA * x + (1.0 + gate2[:, None, :]) * y, ln_g[layer, 1], ln_b[layer, 1])
    return x
```

```python
import functools
import math

import jax
import jax.numpy as jnp
from jax import lax
from jax.experimental import pallas as pl
from jax.experimental.pallas import tpu as pltpu

F32 = jnp.float32
BF16 = jnp.bfloat16

LN_EPS = 1e-5
GLA_HEADS = 4
GLA_GATE_RANK = 16
GLA_GATE_TAU = 16.0
SGU_CHUNK = 128
TOP_K = 4
SWIGLU_LIMIT = 7.0
SWIGLU_ALPHA = 1.702

LANES = 128
SUBLANES = 8
VMEM_LIMIT_BYTES = 56 * 1024 * 1024

ROW_TILE = 256
MOE_ROW_TILE = 256
DMA_ROW_TILE = 256
SCAN_CHUNK = 64
SCAN_GROUP = SUBLANES
SCAN_BLOCK = 512


def _cparams(*sem):
    return pltpu.CompilerParams(dimension_semantics=sem, vmem_limit_bytes=VMEM_LIMIT_BYTES)


def _split_bf16(a):
    hi = a.astype(BF16)
    lo = (a - hi.astype(F32)).astype(BF16)
    return hi, lo


def _dot(a, b):
    return jnp.dot(a, b, preferred_element_type=F32)


def _dot3(a, w):
    a_hi, a_lo = _split_bf16(a)
    w_hi, w_lo = _split_bf16(w)
    return _dot(a_hi, w_hi) + _dot(a_hi, w_lo) + _dot(a_lo, w_hi)


def _layer_norm(x, g, b):
    mu = jnp.mean(x, axis=-1, keepdims=True)
    xc = x - mu
    var = jnp.mean(xc * xc, axis=-1, keepdims=True)
    return xc * lax.rsqrt(var + LN_EPS) * g + b


def _sigmoid(x):
    return 1.0 / (1.0 + jnp.exp(-x))


def _adaln_kernel(c_ref, w_ref, b_ref, o_ref):
    c = c_ref[...]
    sc = c * _sigmoid(c)
    o_ref[0] = _dot3(sc, w_ref[0]) + b_ref[0]


def _adaln(c, ada_w, ada_b):
    depth, d, n = ada_w.shape
    bsz = c.shape[0]
    tn = 2048
    return pl.pallas_call(
        _adaln_kernel,
        out_shape=jax.ShapeDtypeStruct((depth, bsz, n), F32),
        grid=(depth, n // tn),
        in_specs=[
            pl.BlockSpec((bsz, d), lambda l, j: (0, 0)),
            pl.BlockSpec((1, d, tn), lambda l, j: (l, 0, j)),
            pl.BlockSpec((1, 1, tn), lambda l, j: (l, 0, j)),
        ],
        out_specs=pl.BlockSpec((1, bsz, tn), lambda l, j: (l, 0, j)),
        compiler_params=_cparams("arbitrary", "arbitrary"),
        name="adaln",
    )(c, ada_w, ada_b.reshape(depth, 1, n))


def _sgu_kernel(x_ref, mod_ref, w_in_ref, b_in_ref, vg_ref, vb_ref, ws_ref, bst_ref, w_out_ref,
                b_out_ref, png_ref, pnb_ref, o_ref, gated_ref, *, alpha, width, groups):
    x = x_ref[...]
    mod = mod_ref[0]
    shift, scale, gate = mod[0:1], mod[1:2], mod[2:3]
    h = (x * (1.0 + scale) + shift).astype(BF16)
    z = _dot(h, w_in_ref[...]) + b_in_ref[...]
    z = 0.5 * z * (1.0 + lax.erf(z * math.sqrt(0.5)))
    u = z[:, :width]
    v = _layer_norm(z[:, width:], vg_ref[...], vb_ref[...]).astype(BF16)
    gd = width // groups
    for n in range(x.shape[0] // SGU_CHUNK):
        rows = slice(n * SGU_CHUNK, (n + 1) * SGU_CHUNK)
        for g in range(groups):
            cols = slice(g * gd, (g + 1) * gd)
            sv = _dot(ws_ref[g], v[rows, cols]) + bst_ref[:, g:g + 1]
            gated_ref[rows, cols] = (u[rows, cols] * sv).astype(BF16)
    y = _dot(gated_ref[...], w_out_ref[...]) + b_out_ref[...]
    o_ref[...] = _layer_norm(alpha * x + (1.0 + gate) * y, png_ref[...], pnb_ref[...])


def _sgu_layer(xt, mod, seq, alpha, w_in, b_in, vg, vb, w_s, b_s, w_out, b_out, png, pnb):
    t, d = xt.shape
    width = w_out.shape[0]
    groups = w_s.shape[0]
    tm = ROW_TILE
    row = lambda i: (i, 0)
    full2 = lambda i: (0, 0)
    return pl.pallas_call(
        functools.partial(_sgu_kernel, alpha=alpha, width=width, groups=groups),
        out_shape=jax.ShapeDtypeStruct((t, d), F32),
        grid=(t // tm,),
        in_specs=[
            pl.BlockSpec((tm, d), row),
            pl.BlockSpec((1, 6, d), lambda i: ((i * tm) // seq, 0, 0)),
            pl.BlockSpec((d, 2 * width), full2),
            pl.BlockSpec((1, 2 * width), full2),
            pl.BlockSpec((1, width), full2),
            pl.BlockSpec((1, width), full2),
            pl.BlockSpec((groups, SGU_CHUNK, SGU_CHUNK), lambda i: (0, 0, 0)),
            pl.BlockSpec((SGU_CHUNK, groups), full2),
            pl.BlockSpec((width, d), full2),
            pl.BlockSpec((1, d), full2),
            pl.BlockSpec((1, d), full2),
            pl.BlockSpec((1, d), full2),
        ],
        out_specs=pl.BlockSpec((tm, d), row),
        scratch_shapes=[pltpu.VMEM((tm, width), BF16)],
        compiler_params=_cparams("arbitrary"),
        name="sgu_mixer",
    )(xt, mod, w_in.astype(BF16), b_in.reshape(1, -1), vg.reshape(1, -1), vb.reshape(1, -1),
      w_s.astype(BF16), b_s.T, w_out.astype(BF16), b_out.reshape(1, -1), png.reshape(1, -1),
      pnb.reshape(1, -1))


def _gla_in_kernel(x_ref, mod_ref, w_ref, wlr_ref, w2_ref, bg_ref, q_ref, k_ref, v_ref, r_ref,
                   gf_ref, gb_ref, *, qk, vo, q_scale):
    x = x_ref[...]
    mod = mod_ref[0]
    shift, scale = mod[0:1], mod[1:2]
    h = (x * (1.0 + scale) + shift).astype(BF16)
    p = _dot(h, w_ref[...])
    q_ref[...] = p[:, :qk] * q_scale
    k_ref[...] = p[:, qk:2 * qk]
    v_ref[...] = p[:, 2 * qk:2 * qk + vo].astype(BF16)
    r_ref[...] = p[:, 2 * qk + vo:]
    lr = _dot(h, wlr_ref[...])
    z = _dot(lr.astype(BF16), w2_ref[...]) + bg_ref[...]
    g = (jnp.minimum(z, 0.0) - jnp.log1p(jnp.exp(-jnp.abs(z)))) * (1.0 / GLA_GATE_TAU)
    gf_ref[...] = g[:, :qk]
    gb_ref[...] = g[:, qk:]


def _gla_in(xt, mod, seq, w_main, w_lr, w2, bg, qk, vo):
    t, d = xt.shape
    tm = ROW_TILE
    row = lambda i: (i, 0)
    full2 = lambda i: (0, 0)
    dk = qk // GLA_HEADS
    return pl.pallas_call(
        functools.partial(_gla_in_kernel, qk=qk, vo=vo, q_scale=dk ** -0.5),
        out_shape=(
            jax.ShapeDtypeStruct((t, qk), F32),
            jax.ShapeDtypeStruct((t, qk), F32),
            jax.ShapeDtypeStruct((t, vo), BF16),
            jax.ShapeDtypeStruct((t, vo), F32),
            jax.ShapeDtypeStruct((t, qk), F32),
            jax.ShapeDtypeStruct((t, qk), F32),
        ),
        grid=(t // tm,),
        in_specs=[
            pl.BlockSpec((tm, d), row),
            pl.BlockSpec((1, 6, d), lambda i: ((i * tm) // seq, 0, 0)),
            pl.BlockSpec(w_main.shape, full2),
            pl.BlockSpec(w_lr.shape, full2),
            pl.BlockSpec(w2.shape, full2),
            pl.BlockSpec(bg.shape, full2),
        ],
        out_specs=(
            pl.BlockSpec((tm, qk), row),
            pl.BlockSpec((tm, qk), row),
            pl.BlockSpec((tm, vo), row),
            pl.BlockSpec((tm, vo), row),
            pl.BlockSpec((tm, qk), row),
            pl.BlockSpec((tm, qk), row),
        ),
        compiler_params=_cparams("arbitrary"),
        name="gla_in",
    )(xt, mod, w_main, w_lr, w2, bg)


def _gla_scan_kernel(q_ref, k_ref, v_ref, g_ref, o_ref, st_ref, *, reverse, heads, dk, dv):
    c_len, grp = SCAN_CHUNK, SCAN_GROUP
    ngrp = c_len // grp
    nchunk = q_ref.shape[0] // c_len

    @pl.when(pl.program_id(1) == 0)
    def _():
        st_ref[...] = jnp.zeros_like(st_ref)

    row = lax.broadcasted_iota(jnp.int32, (c_len, c_len), 0)
    col = lax.broadcasted_iota(jnp.int32, (c_len, c_len), 1)
    later = (col >= row) if reverse else (col <= row)
    tri = jnp.where(later, 1.0, 0.0).astype(BF16)
    grp_r, grp_c = row // grp, col // grp
    off_mask = (grp_r < grp_c) if reverse else (grp_r > grp_c)
    row_grp = lax.broadcasted_iota(jnp.int32, (c_len, 1), 0) // grp
    gi = lax.broadcasted_iota(jnp.int32, (ngrp, grp, c_len), 0)
    ii = lax.broadcasted_iota(jnp.int32, (ngrp, grp, c_len), 1)
    lane = lax.broadcasted_iota(jnp.int32, (ngrp, grp, c_len), 2)
    diag_masks = []
    for jj in range(grp):
        tri_ok = (ii <= jj) if reverse else (ii >= jj)
        diag_masks.append((lane == gi * grp + jj) & tri_ok)
    edge_in_grp = 0 if reverse else grp - 1
    edge_in_chunk = 0 if reverse else c_len - 1
    nt_dims = (((1,), (1,)), ((), ()))
    tn_dims = (((0,), (0,)), ((), ()))

    def chunk_body(ci, carry):
        c = (nchunk - 1 - ci) if reverse else ci
        rows = pl.ds(pl.multiple_of(c * c_len, c_len), c_len)
        for hd in range(heads):
            kcols = slice(hd * dk, (hd + 1) * dk)
            vcols = slice(hd * dv, (hd + 1) * dv)
            g = g_ref[rows, kcols]
            q = q_ref[rows, kcols]
            k = k_ref[rows, kcols]
            v = v_ref[rows, vcols]
            g_hi, g_lo = _split_bf16(g)
            b = _dot(tri, g_hi) + _dot(tri, g_lo)
            b3 = b.reshape(ngrp, grp, dk)
            q3 = q.reshape(ngrp, grp, dk)
            k3 = k.reshape(ngrp, grp, dk)
            e_all = b3[:, edge_in_grp:edge_in_grp + 1, :]
            e_own = jnp.broadcast_to(e_all, b3.shape).reshape(c_len, dk)
            b_edge = b[edge_in_chunk:edge_in_chunk + 1, :]

            s_t = st_ref[hd]
            qi = (q * jnp.exp(b)).astype(BF16)
            o = lax.dot_general(qi, s_t.astype(BF16), nt_dims, preferred_element_type=F32)

            ktil = k * jnp.exp(e_own - b)
            qhat, khat = [], []
            for j in range(ngrp):
                e_j = e_all[j]
                qhat.append((q * jnp.exp(jnp.minimum(b - e_j, 0.0))).astype(BF16))
                khat.append(jnp.where(row_grp == j, ktil, 0.0).astype(BF16))
            a_off = lax.dot_general(jnp.concatenate(qhat, axis=1), jnp.concatenate(khat, axis=1),
                                    nt_dims, preferred_element_type=F32)
            a3 = jnp.where(off_mask, a_off, 0.0).reshape(ngrp, grp, c_len)

            for jj in range(grp):
                kb = k3[:, jj:jj + 1, :]
                bb = b3[:, jj:jj + 1, :]
                t = q3 * kb * jnp.exp(jnp.minimum(b3 - bb, 0.0))
                s = jnp.sum(t, axis=-1, keepdims=True)
                a3 = jnp.where(diag_masks[jj], s, a3)
            a = a3.reshape(c_len, c_len).astype(BF16)
            o = o + _dot(a, v)
            o_ref[rows, vcols] = o

            kst = (k * jnp.exp(b_edge - b)).astype(BF16)
            upd = lax.dot_general(v, kst, tn_dims, preferred_element_type=F32)
            st_ref[hd] = s_t * jnp.exp(b_edge) + upd
        return carry

    lax.fori_loop(0, nchunk, chunk_body, 0)


def _gla_scan(q, k, v, g, bsz, seq, reverse):
    t, qk = q.shape
    vo = v.shape[1]
    dk, dv = qk // GLA_HEADS, vo // GLA_HEADS
    sb = min(SCAN_BLOCK, seq)
    nsb = seq // sb
    if reverse:
        blk = lambda b, s: (b * nsb + (nsb - 1 - s), 0)
    else:
        blk = lambda b, s: (b * nsb + s, 0)
    return pl.pallas_call(
        functools.partial(_gla_scan_kernel, reverse=reverse, heads=GLA_HEADS, dk=dk, dv=dv),
        out_shape=jax.ShapeDtypeStruct((t, vo), F32),
        grid=(bsz, nsb),
        in_specs=[
            pl.BlockSpec((sb, qk), blk),
            pl.BlockSpec((sb, qk), blk),
            pl.BlockSpec((sb, vo), blk),
            pl.BlockSpec((sb, qk), blk),
        ],
        out_specs=pl.BlockSpec((sb, vo), blk),
        scratch_shapes=[pltpu.VMEM((GLA_HEADS, dv, dk), F32)],
        compiler_params=_cparams("arbitrary", "arbitrary"),
        name="gla_scan_bwd" if reverse else "gla_scan_fwd",
    )(q, k, v, g)


def _gla_out_kernel(x_ref, mod_ref, of_ref, ob_ref, r_ref, ng_ref, w_out_ref, png_ref, pnb_ref,
                    o_ref, *, alpha, heads):
    x = x_ref[...]
    gate = mod_ref[0][2:3]
    o = of_ref[...] + ob_ref[...]
    dv = o.shape[1] // heads
    pieces = []
    for hd in range(heads):
        oh = o[:, hd * dv:(hd + 1) * dv]
        ms = jnp.mean(oh * oh, axis=-1, keepdims=True)
        pieces.append(oh * lax.rsqrt(ms + LN_EPS) * ng_ref[...])
    on = jnp.concatenate(pieces, axis=1)
    r = r_ref[...]
    gated = (on * (r * _sigmoid(r))).astype(BF16)
    y = _dot(gated, w_out_ref[...])
    o_ref[...] = _layer_norm(alpha * x + (1.0 + gate) * y, png_ref[...], pnb_ref[...])


def _gla_out(xt, mod, seq, alpha, o_f, o_b, r, norm_g, w_out, png, pnb):
    t, d = xt.shape
    vo = o_f.shape[1]
    tm = ROW_TILE
    row = lambda i: (i, 0)
    full2 = lambda i: (0, 0)
    return pl.pallas_call(
        functools.partial(_gla_out_kernel, alpha=alpha, heads=GLA_HEADS),
        out_shape=jax.ShapeDtypeStruct((t, d), F32),
        grid=(t // tm,),
        in_specs=[
            pl.BlockSpec((tm, d), row),
            pl.BlockSpec((1, 6, d), lambda i: ((i * tm) // seq, 0, 0)),
            pl.BlockSpec((tm, vo), row),
            pl.BlockSpec((tm, vo), row),
            pl.BlockSpec((tm, vo), row),
            pl.BlockSpec((1, vo // GLA_HEADS), full2),
            pl.BlockSpec((vo, d), full2),
            pl.BlockSpec((1, d), full2),
            pl.BlockSpec((1, d), full2),
        ],
        out_specs=pl.BlockSpec((tm, d), row),
        compiler_params=_cparams("arbitrary"),
        name="gla_out",
    )(xt, mod, o_f, o_b, r, norm_g.reshape(1, -1), w_out.astype(BF16), png.reshape(1, -1),
      pnb.reshape(1, -1))


def _gla_layer(xt, mod, bsz, seq, alpha, w_in, wg2_f, bg_f, wg2_b, bg_b, norm_g, w_out, png, pnb):
    rank = GLA_GATE_RANK
    qk = wg2_f.shape[1]
    vo = w_out.shape[0]
    n_main = 2 * qk + 2 * vo
    w_main = w_in[:, :n_main].astype(BF16)
    w_lr = w_in[:, n_main:].astype(BF16)
    zeros = jnp.zeros((rank, qk), F32)
    w2 = jnp.concatenate([jnp.concatenate([wg2_f, zeros], axis=1),
                          jnp.concatenate([zeros, wg2_b], axis=1)], axis=0).astype(BF16)
    bg = jnp.concatenate([bg_f, bg_b]).reshape(1, -1)
    q, k, v, r, gf, gb = _gla_in(xt, mod, seq, w_main, w_lr, w2, bg, qk, vo)
    o_f = _gla_scan(q, k, v, gf, bsz, seq, reverse=False)
    o_b = _gla_scan(q, k, v, gb, bsz, seq, reverse=True)
    return _gla_out(xt, mod, seq, alpha, o_f, o_b, r, norm_g, w_out, png, pnb)


def _route_kernel(x_ref, mod_ref, wr_ref, br_ref, h_ref, meta_ref, cnt_ref, carry_ref, *, n_exp):
    i = pl.program_id(0)

    @pl.when(i == 0)
    def _():
        carry_ref[...] = jnp.zeros_like(carry_ref)

    x = x_ref[...]
    mod = mod_ref[0]
    shift, scale = mod[3:4], mod[4:5]
    h = x * (1.0 + scale) + shift
    h_ref[...] = h
    tm = x.shape[0]
    lane = lax.broadcasted_iota(jnp.int32, (tm, LANES), 1)
    logits = _dot3(h, wr_ref[...]) + br_ref[...]
    neg_inf = jnp.float32(-jnp.inf)
    cur = jnp.where(lane < n_exp, logits, neg_inf)
    vals, idxs, hots = [], [], []
    for _ in range(TOP_K):
        m = jnp.max(cur, axis=-1, keepdims=True)
        idx = jnp.min(jnp.where(cur == m, lane, LANES), axis=-1, keepdims=True)
        hot = lane == idx
        cur = jnp.where(hot, neg_inf, cur)
        vals.append(m)
        idxs.append(idx)
        hots.append(hot)
    ps = [jnp.exp(vk - vals[0]) for vk in vals]
    den = ps[0] + ps[1] + ps[2] + ps[3]
    ws = [p / den for p in ps]

    cnt = jnp.zeros((tm, LANES), F32)
    for hot in hots:
        cnt = cnt + jnp.where(hot, 1.0, 0.0)
    r_i = lax.broadcasted_iota(jnp.int32, (tm, tm), 0)
    c_i = lax.broadcasted_iota(jnp.int32, (tm, tm), 1)
    strict = jnp.where(c_i < r_i, 1.0, 0.0).astype(BF16)
    carry = carry_ref[0:1, :]
    before = _dot(strict, cnt.astype(BF16)) + carry
    meta = jnp.zeros((tm, LANES), F32)
    for kk in range(TOP_K):
        rank = jnp.sum(jnp.where(hots[kk], before, 0.0), axis=-1, keepdims=True)
        meta = jnp.where(lane == kk, idxs[kk].astype(F32), meta)
        meta = jnp.where(lane == TOP_K + kk, ws[kk], meta)
        meta = jnp.where(lane == 2 * TOP_K + kk, rank, meta)
    meta_ref[...] = meta
    new_carry = carry + jnp.sum(cnt, axis=0, keepdims=True)
    carry_ref[...] = jnp.broadcast_to(new_carry, carry_ref.shape)
    cnt_ref[...] = jnp.broadcast_to(new_carry, cnt_ref.shape)


def _route(x1, mod, seq, w_router, b_router):
    t, d = x1.shape
    n_exp = w_router.shape[1]
    tm = ROW_TILE
    wr = jnp.pad(w_router, ((0, 0), (0, LANES - n_exp)))
    br = jnp.pad(b_router, (0, LANES - n_exp)).reshape(1, LANES)
    row = lambda i: (i, 0)
    full2 = lambda i: (0, 0)
    return pl.pallas_call(
        functools.partial(_route_kernel, n_exp=n_exp),
        out_shape=(
            jax.ShapeDtypeStruct((t, d), F32),
            jax.ShapeDtypeStruct((t, LANES), F32),
            jax.ShapeDtypeStruct((SUBLANES, LANES), F32),
        ),
        grid=(t // tm,),
        in_specs=[
            pl.BlockSpec((tm, d), row),
            pl.BlockSpec((1, 6, d), lambda i: ((i * tm) // seq, 0, 0)),
            pl.BlockSpec((d, LANES), full2),
            pl.BlockSpec((1, LANES), full2),
        ],
        out_specs=(
            pl.BlockSpec((tm, d), row),
            pl.BlockSpec((tm, LANES), row),
            pl.BlockSpec((SUBLANES, LANES), full2),
        ),
        scratch_shapes=[pltpu.VMEM((SUBLANES, LANES), F32)],
        compiler_params=_cparams("arbitrary"),
        name="moe_route",
    )(x1, mod, wr, br)


def _scatter_kernel(dest_ref, h_hbm, xs_in_hbm, xs_hbm, sem, *, tm):
    del xs_in_hbm
    base = pl.program_id(0) * tm

    def body(r, carry):
        for kk in range(TOP_K):
            d = dest_ref[0, 0, r * TOP_K + kk]
            pltpu.make_async_copy(h_hbm.at[pl.ds(base + r, 1)], xs_hbm.at[pl.ds(d, 1)], sem).start()
        return carry

    lax.fori_loop(0, tm, body, 0)
    pltpu.make_async_copy(h_hbm.at[pl.ds(0, tm * TOP_K)], xs_hbm.at[pl.ds(0, tm * TOP_K)], sem).wait()


def _scatter_rows(h, dest, cap):
    t, d = h.shape
    tm = DMA_ROW_TILE
    xs0 = jnp.zeros((cap, d), F32)
    return pl.pallas_call(
        functools.partial(_scatter_kernel, tm=tm),
        out_shape=jax.ShapeDtypeStruct((cap, d), F32),
        grid=(t // tm,),
        in_specs=[
            pl.BlockSpec((1, 1, tm * TOP_K), lambda i: (i, 0, 0), memory_space=pltpu.SMEM),
            pl.BlockSpec(memory_space=pl.ANY),
            pl.BlockSpec(memory_space=pl.ANY),
        ],
        out_specs=pl.BlockSpec(memory_space=pl.ANY),
        scratch_shapes=[pltpu.SemaphoreType.DMA(())],
        input_output_aliases={2: 0},
        compiler_params=_cparams("arbitrary"),
        name="moe_scatter",
    )(dest.reshape(t // tm, 1, tm * TOP_K), h, xs0)


def _expert_kernel(blk_e_ref, nused_ref, xs_ref, wu_ref, bu_ref, wd_ref, bd_ref, ys_ref, *, d_exp):
    del blk_e_ref
    used = pl.program_id(0) < nused_ref[0]

    @pl.when(jnp.logical_not(used))
    def _():
        ys_ref[...] = jnp.zeros_like(ys_ref)

    @pl.when(used)
    def _():
        x = xs_ref[...].astype(BF16)
        hu = _dot(x, wu_ref[0]) + bu_ref[0]
        gate = jnp.minimum(hu[:, :d_exp], SWIGLU_LIMIT)
        lin = jnp.clip(hu[:, d_exp:], -SWIGLU_LIMIT, SWIGLU_LIMIT)
        act = gate * _sigmoid(SWIGLU_ALPHA * gate) * (lin + 1.0)
        ys_ref[...] = _dot(act.astype(BF16), wd_ref[0]) + bd_ref[0]


def _expert_ffn(xs, blk_e, nused, w_up, b_up, w_down, b_down):
    cap, d = xs.shape
    n_exp, _, two_de = w_up.shape
    d_exp = two_de // 2
    tm = MOE_ROW_TILE
    nblk = cap // tm
    rows = lambda i, be, nu: (jnp.minimum(i, nu[0] - 1), 0)
    by_e = lambda i, be, nu: (be[i], 0, 0)
    return pl.pallas_call(
        functools.partial(_expert_kernel, d_exp=d_exp),
        out_shape=jax.ShapeDtypeStruct((cap, d), F32),
        grid_spec=pltpu.PrefetchScalarGridSpec(
            num_scalar_prefetch=2,
            grid=(nblk,),
            in_specs=[
                pl.BlockSpec((tm, d), rows),
                pl.BlockSpec((1, d, two_de), by_e),
                pl.BlockSpec((1, 1, two_de), by_e),
                pl.BlockSpec((1, d_exp, d), by_e),
                pl.BlockSpec((1, 1, d), by_e),
            ],
            out_specs=pl.BlockSpec((tm, d), lambda i, be, nu: (i, 0)),
        ),
        compiler_params=_cparams("arbitrary"),
        name="moe_experts",
    )(blk_e, nused, xs, w_up, b_up, w_down, b_down)


def _combine_kernel(dest_ref, x_ref, mod_ref, meta_ref, ys_hbm, png_ref, pnb_ref, o_ref, buf, sem, *,
                    alpha):
    tm = x_ref.shape[0]

    def body(r, carry):
        for kk in range(TOP_K):
            d = dest_ref[0, 0, r * TOP_K + kk]
            pltpu.make_async_copy(ys_hbm.at[pl.ds(d, 1)], buf.at[kk, pl.ds(r, 1)], sem).start()
        return carry

    lax.fori_loop(0, tm, body, 0)
    for kk in range(TOP_K):
        pltpu.make_async_copy(ys_hbm.at[pl.ds(0, tm)], buf.at[kk], sem).wait()
    meta = meta_ref[...]
    y = meta[:, TOP_K:TOP_K + 1] * buf[0]
    for kk in range(1, TOP_K):
        y = y + meta[:, TOP_K + kk:TOP_K + kk + 1] * buf[kk]
    x = x_ref[...]
    gate = mod_ref[0][5:6]
    o_ref[...] = _layer_norm(alpha * x + (1.0 + gate) * y, png_ref[...], pnb_ref[...])


def _combine(x1, mod, seq, alpha, meta, dest, ys, png, pnb):
    t, d = x1.shape
    tm = DMA_ROW_TILE
    row = lambda i: (i, 0)
    full2 = lambda i: (0, 0)
    return pl.pallas_call(
        functools.partial(_combine_kernel, alpha=alpha),
        out_shape=jax.ShapeDtypeStruct((t, d), F32),
        grid=(t // tm,),
        in_specs=[
            pl.BlockSpec((1, 1, tm * TOP_K), lambda i: (i, 0, 0), memory_space=pltpu.SMEM),
            pl.BlockSpec((tm, d), row),
            pl.BlockSpec((1, 6, d), lambda i: ((i * tm) // seq, 0, 0)),
            pl.BlockSpec((tm, LANES), row),
            pl.BlockSpec(memory_space=pl.ANY),
            pl.BlockSpec((1, d), full2),
            pl.BlockSpec((1, d), full2),
        ],
        out_specs=pl.BlockSpec((tm, d), row),
        scratch_shapes=[pltpu.VMEM((TOP_K, tm, d), F32), pltpu.SemaphoreType.DMA(())],
        compiler_params=_cparams("arbitrary"),
        name="moe_combine",
    )(dest.reshape(t // tm, 1, tm * TOP_K), x1, mod, meta, ys, png.reshape(1, -1), pnb.reshape(1, -1))


def _moe_layer(x1, mod, seq, alpha, w_router, b_router, w_up, b_up, w_down, b_down, png, pnb):
    t, d = x1.shape
    n_exp = w_router.shape[1]
    tm = MOE_ROW_TILE
    h, meta, cnt = _route(x1, mod, seq, w_router, b_router)

    counts = cnt[0, :n_exp].astype(jnp.int32)
    padded = (counts + tm - 1) // tm * tm
    pad_end = jnp.cumsum(padded)
    pad_start = pad_end - padded
    idx = meta[:, :TOP_K].astype(jnp.int32)
    rank = meta[:, 2 * TOP_K:3 * TOP_K].astype(jnp.int32)
    dest = pad_start[idx] + rank
    nblk = -(-(t * TOP_K + n_exp * tm) // tm)
    blk_start = jnp.arange(nblk, dtype=jnp.int32) * tm
    blk_e = jnp.minimum(jnp.searchsorted(pad_end, blk_start, side="right"), n_exp - 1).astype(jnp.int32)
    nused = (pad_end[-1:] // tm).astype(jnp.int32)

    xs = _scatter_rows(h, dest, nblk * tm)
    wu = jnp.concatenate([w_up[:, :, 0::2], w_up[:, :, 1::2]], axis=-1).astype(BF16)
    bu = jnp.concatenate([b_up[:, 0::2], b_up[:, 1::2]], axis=-1).reshape(n_exp, 1, -1)
    ys = _expert_ffn(xs, blk_e, nused, wu, bu, w_down.astype(BF16), b_down.reshape(n_exp, 1, -1))
    return _combine(x1, mod, seq, alpha, meta, dest, ys, png, pnb)


def kernel(x, c, ada_w, ada_b, ln_g, ln_b, gla_w_in, gla_wg2_f, gla_bg_f, gla_wg2_b, gla_bg_b, gla_norm_g, gla_w_out, sgu_w_in, sgu_b_in, sgu_ln_g, sgu_ln_b, sgu_w_s, sgu_b_s, sgu_w_out, sgu_b_out, moe_w_router, moe_b_router, moe_w_up, moe_b_up, moe_w_down, moe_b_down):
    bsz, seq, d = x.shape
    depth = ada_w.shape[0]
    alpha = (2.0 * depth) ** 0.25
    mod_all = _adaln(c, ada_w, ada_b).reshape(depth, bsz, 6, d)
    xt = x.reshape(bsz * seq, d)
    for layer in range(depth):
        mod = mod_all[layer]
        i = layer // 2
        if layer % 2 == 0:
            x1 = _gla_layer(xt, mod, bsz, seq, alpha, gla_w_in[i], gla_wg2_f[i], gla_bg_f[i],
                            gla_wg2_b[i], gla_bg_b[i], gla_norm_g[i], gla_w_out[i],
                            ln_g[layer, 0], ln_b[layer, 0])
        else:
            x1 = _sgu_layer(xt, mod, seq, alpha, sgu_w_in[i], sgu_b_in[i], sgu_ln_g[i], sgu_ln_b[i],
                            sgu_w_s[i], sgu_b_s[i], sgu_w_out[i], sgu_b_out[i],
                            ln_g[layer, 0], ln_b[layer, 0])
        xt = _moe_layer(x1, mod, seq, alpha, moe_w_router[layer], moe_b_router[layer],
                        moe_w_up[layer], moe_b_up[layer], moe_w_down[layer], moe_b_down[layer],
                        ln_g[layer, 1], ln_b[layer, 1])
    return xt.reshape(bsz, seq, d)
```

```python
import functools
import math

import jax
import jax.numpy as jnp
from jax import lax
from jax.experimental import pallas as pl
from jax.experimental.pallas import tpu as pltpu

F32 = jnp.float32
BF16 = jnp.bfloat16

LN_EPS = 1e-5
GLA_HEADS = 4
GLA_GATE_RANK = 16
GLA_GATE_TAU = 16.0
SGU_CHUNK = 128
TOP_K = 4
SWIGLU_LIMIT = 7.0
SWIGLU_ALPHA = 1.702

LANES = 128
SUBLANES = 8
VMEM_LIMIT_BYTES = 56 * 1024 * 1024

ROW_TILE = 256
MOE_ROW_TILE = 256
DMA_ROW_TILE = 256
SCAN_CHUNK = 64
SCAN_GROUP = SUBLANES
SCAN_BLOCK = 512


def _cparams(*sem):
    return pltpu.CompilerParams(dimension_semantics=sem, vmem_limit_bytes=VMEM_LIMIT_BYTES)


def _split_bf16(a):
    hi = a.astype(BF16)
    lo = (a - hi.astype(F32)).astype(BF16)
    return hi, lo


def _dot(a, b):
    return jnp.dot(a, b, preferred_element_type=F32)


def _dot3(a, w):
    a_hi, a_lo = _split_bf16(a)
    w_hi, w_lo = _split_bf16(w)
    return _dot(a_hi, w_hi) + _dot(a_hi, w_lo) + _dot(a_lo, w_hi)


def _layer_norm(x, g, b):
    mu = jnp.mean(x, axis=-1, keepdims=True)
    xc = x - mu
    var = jnp.mean(xc * xc, axis=-1, keepdims=True)
    return xc * lax.rsqrt(var + LN_EPS) * g + b


def _sigmoid(x):
    return 1.0 / (1.0 + jnp.exp(-x))


def _adaln_kernel(c_ref, w_ref, b_ref, o_ref):
    c = c_ref[...]
    sc = c * _sigmoid(c)
    o_ref[0] = _dot3(sc, w_ref[0]) + b_ref[0]


def _adaln(c, ada_w, ada_b):
    depth, d, n = ada_w.shape
    bsz = c.shape[0]
    tn = 2048
    return pl.pallas_call(
        _adaln_kernel,
        out_shape=jax.ShapeDtypeStruct((depth, bsz, n), F32),
        grid=(depth, n // tn),
        in_specs=[
            pl.BlockSpec((bsz, d), lambda l, j: (0, 0)),
            pl.BlockSpec((1, d, tn), lambda l, j: (l, 0, j)),
            pl.BlockSpec((1, 1, tn), lambda l, j: (l, 0, j)),
        ],
        out_specs=pl.BlockSpec((1, bsz, tn), lambda l, j: (l, 0, j)),
        compiler_params=_cparams("arbitrary", "arbitrary"),
        name="adaln",
    )(c, ada_w, ada_b.reshape(depth, 1, n))


def _sgu_kernel(x_ref, mod_ref, w_in_ref, b_in_ref, vg_ref, vb_ref, ws_ref, bst_ref, w_out_ref,
                b_out_ref, png_ref, pnb_ref, o_ref, gated_ref, *, alpha, width, groups):
    x = x_ref[...]
    mod = mod_ref[0]
    shift, scale, gate = mod[0:1], mod[1:2], mod[2:3]
    h = (x * (1.0 + scale) + shift).astype(BF16)
    z = _dot(h, w_in_ref[...]) + b_in_ref[...]
    z = 0.5 * z * (1.0 + lax.erf(z * math.sqrt(0.5)))
    u = z[:, :width]
    v = _layer_norm(z[:, width:], vg_ref[...], vb_ref[...]).astype(BF16)
    gd = width // groups
    for n in range(x.shape[0] // SGU_CHUNK):
        rows = slice(n * SGU_CHUNK, (n + 1) * SGU_CHUNK)
        for g in range(groups):
            cols = slice(g * gd, (g + 1) * gd)
            sv = _dot(ws_ref[g], v[rows, cols]) + bst_ref[:, g:g + 1]
            gated_ref[rows, cols] = (u[rows, cols] * sv).astype(BF16)
    y = _dot(gated_ref[...], w_out_ref[...]) + b_out_ref[...]
    o_ref[...] = _layer_norm(alpha * x + (1.0 + gate) * y, png_ref[...], pnb_ref[...])


def _sgu_layer(xt, mod, seq, alpha, w_in, b_in, vg, vb, w_s, b_s, w_out, b_out, png, pnb):
    t, d = xt.shape
    width = w_out.shape[0]
    groups = w_s.shape[0]
    tm = ROW_TILE
    row = lambda i: (i, 0)
    full2 = lambda i: (0, 0)
    return pl.pallas_call(
        functools.partial(_sgu_kernel, alpha=alpha, width=width, groups=groups),
        out_shape=jax.ShapeDtypeStruct((t, d), F32),
        grid=(t // tm,),
        in_specs=[
            pl.BlockSpec((tm, d), row),
            pl.BlockSpec((1, 6, d), lambda i: ((i * tm) // seq, 0, 0)),
            pl.BlockSpec((d, 2 * width), full2),
            pl.BlockSpec((1, 2 * width), full2),
            pl.BlockSpec((1, width), full2),
            pl.BlockSpec((1, width), full2),
            pl.BlockSpec((groups, SGU_CHUNK, SGU_CHUNK), lambda i: (0, 0, 0)),
            pl.BlockSpec((SGU_CHUNK, groups), full2),
            pl.BlockSpec((width, d), full2),
            pl.BlockSpec((1, d), full2),
            pl.BlockSpec((1, d), full2),
            pl.BlockSpec((1, d), full2),
        ],
        out_specs=pl.BlockSpec((tm, d), row),
        scratch_shapes=[pltpu.VMEM((tm, width), BF16)],
        compiler_params=_cparams("arbitrary"),
        name="sgu_mixer",
    )(xt, mod, w_in.astype(BF16), b_in.reshape(1, -1), vg.reshape(1, -1), vb.reshape(1, -1),
      w_s.astype(BF16), b_s.T, w_out.astype(BF16), b_out.reshape(1, -1), png.reshape(1, -1),
      pnb.reshape(1, -1))


def _gla_in_kernel(x_ref, mod_ref, w_ref, wlr_ref, w2_ref, bg_ref, q_ref, k_ref, v_ref, r_ref,
                   gf_ref, gb_ref, *, qk, vo, q_scale):
    x = x_ref[...]
    mod = mod_ref[0]
    shift, scale = mod[0:1], mod[1:2]
    h = (x * (1.0 + scale) + shift).astype(BF16)
    p = _dot(h, w_ref[...])
    q_ref[...] = p[:, :qk] * q_scale
    k_ref[...] = p[:, qk:2 * qk]
    v_ref[...] = p[:, 2 * qk:2 * qk + vo].astype(BF16)
    r_ref[...] = p[:, 2 * qk + vo:]
    lr = _dot(h, wlr_ref[...])
    z = _dot(lr.astype(BF16), w2_ref[...]) + bg_ref[...]
    g = (jnp.minimum(z, 0.0) - jnp.log1p(jnp.exp(-jnp.abs(z)))) * (1.0 / GLA_GATE_TAU)
    gf_ref[...] = g[:, :qk]
    gb_ref[...] = g[:, qk:]


def _gla_in(xt, mod, seq, w_main, w_lr, w2, bg, qk, vo):
    t, d = xt.shape
    tm = ROW_TILE
    row = lambda i: (i, 0)
    full2 = lambda i: (0, 0)
    dk = qk // GLA_HEADS
    return pl.pallas_call(
        functools.partial(_gla_in_kernel, qk=qk, vo=vo, q_scale=dk ** -0.5),
        out_shape=(
            jax.ShapeDtypeStruct((t, qk), F32),
            jax.ShapeDtypeStruct((t, qk), F32),
            jax.ShapeDtypeStruct((t, vo), BF16),
            jax.ShapeDtypeStruct((t, vo), F32),
            jax.ShapeDtypeStruct((t, qk), F32),
            jax.ShapeDtypeStruct((t, qk), F32),
        ),
        grid=(t // tm,),
        in_specs=[
            pl.BlockSpec((tm, d), row),
            pl.BlockSpec((1, 6, d), lambda i: ((i * tm) // seq, 0, 0)),
            pl.BlockSpec(w_main.shape, full2),
            pl.BlockSpec(w_lr.shape, full2),
            pl.BlockSpec(w2.shape, full2),
            pl.BlockSpec(bg.shape, full2),
        ],
        out_specs=(
            pl.BlockSpec((tm, qk), row),
            pl.BlockSpec((tm, qk), row),
            pl.BlockSpec((tm, vo), row),
            pl.BlockSpec((tm, vo), row),
            pl.BlockSpec((tm, qk), row),
            pl.BlockSpec((tm, qk), row),
        ),
        compiler_params=_cparams("arbitrary"),
        name="gla_in",
    )(xt, mod, w_main, w_lr, w2, bg)


def _gla_scan_kernel(q_ref, k_ref, v_ref, g_ref, o_ref, st_ref, *, reverse, heads, dk, dv):
    c_len, grp = SCAN_CHUNK, SCAN_GROUP
    ngrp = c_len // grp
    nchunk = q_ref.shape[0] // c_len

    @pl.when(pl.program_id(1) == 0)
    def _():
        st_ref[...] = jnp.zeros_like(st_ref)

    row = lax.broadcasted_iota(jnp.int32, (c_len, c_len), 0)
    col = lax.broadcasted_iota(jnp.int32, (c_len, c_len), 1)
    later = (col >= row) if reverse else (col <= row)
    tri = jnp.where(later, 1.0, 0.0).astype(BF16)
    grp_r, grp_c = row // grp, col // grp
    off_mask = (grp_r < grp_c) if reverse else (grp_r > grp_c)
    row_grp = lax.broadcasted_iota(jnp.int32, (c_len, 1), 0) // grp
    gi = lax.broadcasted_iota(jnp.int32, (ngrp, grp, c_len), 0)
    ii = lax.broadcasted_iota(jnp.int32, (ngrp, grp, c_len), 1)
    lane = lax.broadcasted_iota(jnp.int32, (ngrp, grp, c_len), 2)
    diag_masks = []
    for jj in range(grp):
        tri_ok = (ii <= jj) if reverse else (ii >= jj)
        diag_masks.append((lane == gi * grp + jj) & tri_ok)
    edge_in_grp = 0 if reverse else grp - 1
    edge_in_chunk = 0 if reverse else c_len - 1
    nt_dims = (((1,), (1,)), ((), ()))
    tn_dims = (((0,), (0,)), ((), ()))

    def chunk_body(ci, carry):
        c = (nchunk - 1 - ci) if reverse else ci
        rows = pl.ds(pl.multiple_of(c * c_len, c_len), c_len)
        for hd in range(heads):
            kcols = slice(hd * dk, (hd + 1) * dk)
            vcols = slice(hd * dv, (hd + 1) * dv)
            g = g_ref[rows, kcols]
            q = q_ref[rows, kcols]
            k = k_ref[rows, kcols]
            v = v_ref[rows, vcols]
            g_hi, g_lo = _split_bf16(g)
            b = _dot(tri, g_hi) + _dot(tri, g_lo)
            b3 = b.reshape(ngrp, grp, dk)
            q3 = q.reshape(ngrp, grp, dk)
            k3 = k.reshape(ngrp, grp, dk)
            e_all = b3[:, edge_in_grp:edge_in_grp + 1, :]
            e_own = jnp.broadcast_to(e_all, b3.shape).reshape(c_len, dk)
            b_edge = b[edge_in_chunk:edge_in_chunk + 1, :]

            s_t = st_ref[hd]
            qi = (q * jnp.exp(b)).astype(BF16)
            o = lax.dot_general(qi, s_t.astype(BF16), nt_dims, preferred_element_type=F32)

            ktil = k * jnp.exp(e_own - b)
            qhat, khat = [], []
            for j in range(ngrp):
                e_j = e_all[j]
                qhat.append((q * jnp.exp(jnp.minimum(b - e_j, 0.0))).astype(BF16))
                khat.append(jnp.where(row_grp == j, ktil, 0.0).astype(BF16))
            a_off = lax.dot_general(jnp.concatenate(qhat, axis=1), jnp.concatenate(khat, axis=1),
                                    nt_dims, preferred_element_type=F32)
            a3 = jnp.where(off_mask, a_off, 0.0).reshape(ngrp, grp, c_len)

            for jj in range(grp):
                kb = k3[:, jj:jj + 1, :]
                bb = b3[:, jj:jj + 1, :]
                t = q3 * kb * jnp.exp(jnp.minimum(b3 - bb, 0.0))
                s = jnp.sum(t, axis=-1, keepdims=True)
                a3 = jnp.where(diag_masks[jj], s, a3)
            a = a3.reshape(c_len, c_len).astype(BF16)
            o = o + _dot(a, v)
            o_ref[rows, vcols] = o

            kst = (k * jnp.exp(b_edge - b)).astype(BF16)
            upd = lax.dot_general(v, kst, tn_dims, preferred_element_type=F32)
            st_ref[hd] = s_t * jnp.exp(b_edge) + upd
        return carry

    lax.fori_loop(0, nchunk, chunk_body, 0)


def _gla_scan(q, k, v, g, bsz, seq, reverse):
    t, qk = q.shape
    vo = v.shape[1]
    dk, dv = qk // GLA_HEADS, vo // GLA_HEADS
    sb = min(SCAN_BLOCK, seq)
    nsb = seq // sb
    if reverse:
        blk = lambda b, s: (b * nsb + (nsb - 1 - s), 0)
    else:
        blk = lambda b, s: (b * nsb + s, 0)
    return pl.pallas_call(
        functools.partial(_gla_scan_kernel, reverse=reverse, heads=GLA_HEADS, dk=dk, dv=dv),
        out_shape=jax.ShapeDtypeStruct((t, vo), F32),
        grid=(bsz, nsb),
        in_specs=[
            pl.BlockSpec((sb, qk), blk),
            pl.BlockSpec((sb, qk), blk),
            pl.BlockSpec((sb, vo), blk),
            pl.BlockSpec((sb, qk), blk),
        ],
        out_specs=pl.BlockSpec((sb, vo), blk),
        scratch_shapes=[pltpu.VMEM((GLA_HEADS, dv, dk), F32)],
        compiler_params=_cparams("arbitrary", "arbitrary"),
        name="gla_scan_bwd" if reverse else "gla_scan_fwd",
    )(q, k, v, g)


def _gla_out_kernel(x_ref, mod_ref, of_ref, ob_ref, r_ref, ng_ref, w_out_ref, png_ref, pnb_ref,
                    o_ref, *, alpha, heads):
    x = x_ref[...]
    gate = mod_ref[0][2:3]
    o = of_ref[...] + ob_ref[...]
    dv = o.shape[1] // heads
    pieces = []
    for hd in range(heads):
        oh = o[:, hd * dv:(hd + 1) * dv]
        ms = jnp.mean(oh * oh, axis=-1, keepdims=True)
        pieces.append(oh * lax.rsqrt(ms + LN_EPS) * ng_ref[...])
    on = jnp.concatenate(pieces, axis=1)
    r = r_ref[...]
    gated = (on * (r * _sigmoid(r))).astype(BF16)
    y = _dot(gated, w_out_ref[...])
    o_ref[...] = _layer_norm(alpha * x + (1.0 + gate) * y, png_ref[...], pnb_ref[...])


def _gla_out(xt, mod, seq, alpha, o_f, o_b, r, norm_g, w_out, png, pnb):
    t, d = xt.shape
    vo = o_f.shape[1]
    tm = ROW_TILE
    row = lambda i: (i, 0)
    full2 = lambda i: (0, 0)
    return pl.pallas_call(
        functools.partial(_gla_out_kernel, alpha=alpha, heads=GLA_HEADS),
        out_shape=jax.ShapeDtypeStruct((t, d), F32),
        grid=(t // tm,),
        in_specs=[
            pl.BlockSpec((tm, d), row),
            pl.BlockSpec((1, 6, d), lambda i: ((i * tm) // seq, 0, 0)),
            pl.BlockSpec((tm, vo), row),
            pl.BlockSpec((tm, vo), row),
            pl.BlockSpec((tm, vo), row),
            pl.BlockSpec((1, vo // GLA_HEADS), full2),
            pl.BlockSpec((vo, d), full2),
            pl.BlockSpec((1, d), full2),
            pl.BlockSpec((1, d), full2),
        ],
        out_specs=pl.BlockSpec((tm, d), row),
        compiler_params=_cparams("arbitrary"),
        name="gla_out",
    )(xt, mod, o_f, o_b, r, norm_g.reshape(1, -1), w_out.astype(BF16), png.reshape(1, -1),
      pnb.reshape(1, -1))


def _gla_layer(xt, mod, bsz, seq, alpha, w_in, wg2_f, bg_f, wg2_b, bg_b, norm_g, w_out, png, pnb):
    rank = GLA_GATE_RANK
    qk = wg2_f.shape[1]
    vo = w_out.shape[0]
    n_main = 2 * qk + 2 * vo
    w_main = w_in[:, :n_main].astype(BF16)
    w_lr = w_in[:, n_main:].astype(BF16)
    zeros = jnp.zeros((rank, qk), F32)
    w2 = jnp.concatenate([jnp.concatenate([wg2_f, zeros], axis=1),
                          jnp.concatenate([zeros, wg2_b], axis=1)], axis=0).astype(BF16)
    bg = jnp.concatenate([bg_f, bg_b]).reshape(1, -1)
    q, k, v, r, gf, gb = _gla_in(xt, mod, seq, w_main, w_lr, w2, bg, qk, vo)
    o_f = _gla_scan(q, k, v, gf, bsz, seq, reverse=False)
    o_b = _gla_scan(q, k, v, gb, bsz, seq, reverse=True)
    return _gla_out(xt, mod, seq, alpha, o_f, o_b, r, norm_g, w_out, png, pnb)


def _route_kernel(x_ref, mod_ref, wr_ref, br_ref, h_ref, meta_ref, cnt_ref, carry_ref, *, n_exp):
    i = pl.program_id(0)

    @pl.when(i == 0)
    def _():
        carry_ref[...] = jnp.zeros_like(carry_ref)

    x = x_ref[...]
    mod = mod_ref[0]
    shift, scale = mod[3:4], mod[4:5]
    h = x * (1.0 + scale) + shift
    h_ref[...] = h
    tm = x.shape[0]
    lane = lax.broadcasted_iota(jnp.int32, (tm, LANES), 1)
    logits = _dot3(h, wr_ref[...]) + br_ref[...]
    neg_inf = jnp.float32(-jnp.inf)
    cur = jnp.where(lane < n_exp, logits, neg_inf)
    vals, idxs, hots = [], [], []
    for _ in range(TOP_K):
        m = jnp.max(cur, axis=-1, keepdims=True)
        idx = jnp.min(jnp.where(cur == m, lane, LANES), axis=-1, keepdims=True)
        hot = lane == idx
        cur = jnp.where(hot, neg_inf, cur)
        vals.append(m)
        idxs.append(idx)
        hots.append(hot)
    ps = [jnp.exp(vk - vals[0]) for vk in vals]
    den = ps[0] + ps[1] + ps[2] + ps[3]
    ws = [p / den for p in ps]

    cnt = jnp.zeros((tm, LANES), F32)
    for hot in hots:
        cnt = cnt + jnp.where(hot, 1.0, 0.0)
    r_i = lax.broadcasted_iota(jnp.int32, (tm, tm), 0)
    c_i = lax.broadcasted_iota(jnp.int32, (tm, tm), 1)
    strict = jnp.where(c_i < r_i, 1.0, 0.0).astype(BF16)
    carry = carry_ref[0:1, :]
    before = _dot(strict, cnt.astype(BF16)) + carry
    meta = jnp.zeros((tm, LANES), F32)
    for kk in range(TOP_K):
        rank = jnp.sum(jnp.where(hots[kk], before, 0.0), axis=-1, keepdims=True)
        meta = jnp.where(lane == kk, idxs[kk].astype(F32), meta)
        meta = jnp.where(lane == TOP_K + kk, ws[kk], meta)
        meta = jnp.where(lane == 2 * TOP_K + kk, rank, meta)
    meta_ref[...] = meta
    new_carry = carry + jnp.sum(cnt, axis=0, keepdims=True)
    carry_ref[...] = jnp.broadcast_to(new_carry, carry_ref.shape)
    cnt_ref[...] = jnp.broadcast_to(new_carry, cnt_ref.shape)


def _route(x1, mod, seq, w_router, b_router):
    t, d = x1.shape
    n_exp = w_router.shape[1]
    tm = ROW_TILE
    wr = jnp.pad(w_router, ((0, 0), (0, LANES - n_exp)))
    br = jnp.pad(b_router, (0, LANES - n_exp)).reshape(1, LANES)
    row = lambda i: (i, 0)
    full2 = lambda i: (0, 0)
    return pl.pallas_call(
        functools.partial(_route_kernel, n_exp=n_exp),
        out_shape=(
            jax.ShapeDtypeStruct((t, d), F32),
            jax.ShapeDtypeStruct((t, LANES), F32),
            jax.ShapeDtypeStruct((SUBLANES, LANES), F32),
        ),
        grid=(t // tm,),
        in_specs=[
            pl.BlockSpec((tm, d), row),
            pl.BlockSpec((1, 6, d), lambda i: ((i * tm) // seq, 0, 0)),
            pl.BlockSpec((d, LANES), full2),
            pl.BlockSpec((1, LANES), full2),
        ],
        out_specs=(
            pl.BlockSpec((tm, d), row),
            pl.BlockSpec((tm, LANES), row),
            pl.BlockSpec((SUBLANES, LANES), full2),
        ),
        scratch_shapes=[pltpu.VMEM((SUBLANES, LANES), F32)],
        compiler_params=_cparams("arbitrary"),
        name="moe_route",
    )(x1, mod, wr, br)


def _scatter_kernel(dest_ref, h_ref, xs_in_hbm, xs_hbm, sem):
    del xs_in_hbm
    tm = h_ref.shape[0]

    def body(r, carry):
        for kk in range(TOP_K):
            d = dest_ref[0, 0, r * TOP_K + kk]
            pltpu.make_async_copy(h_ref.at[pl.ds(r, 1)], xs_hbm.at[pl.ds(d, 1)], sem).start()
        return carry

    lax.fori_loop(0, tm, body, 0)
    for _ in range(TOP_K):
        pltpu.make_async_copy(h_ref, xs_hbm.at[pl.ds(0, tm)], sem).wait()


def _scatter_rows(h, dest, cap):
    t, d = h.shape
    tm = DMA_ROW_TILE
    xs0 = jnp.zeros((cap, d), F32)
    return pl.pallas_call(
        _scatter_kernel,
        out_shape=jax.ShapeDtypeStruct((cap, d), F32),
        grid=(t // tm,),
        in_specs=[
            pl.BlockSpec((1, 1, tm * TOP_K), lambda i: (i, 0, 0), memory_space=pltpu.SMEM),
            pl.BlockSpec((tm, d), lambda i: (i, 0)),
            pl.BlockSpec(memory_space=pl.ANY),
        ],
        out_specs=pl.BlockSpec(memory_space=pl.ANY),
        scratch_shapes=[pltpu.SemaphoreType.DMA(())],
        input_output_aliases={2: 0},
        compiler_params=_cparams("arbitrary"),
        name="moe_scatter",
    )(dest.reshape(t // tm, 1, tm * TOP_K), h, xs0)


def _regroup_kernel(w_ref, p_ref, o_ref):
    half = w_ref.shape[2] // 2
    for cidx in range(w_ref.shape[2] // (2 * LANES)):
        blk = w_ref[0, :, cidx * 2 * LANES:(cidx + 1) * 2 * LANES].astype(BF16)
        t = _dot(blk, p_ref[...]).astype(BF16)
        o_ref[0, :, cidx * LANES:(cidx + 1) * LANES] = t[:, :LANES]
        o_ref[0, :, half + cidx * LANES:half + (cidx + 1) * LANES] = t[:, LANES:]


def _regroup_even_odd(w):
    n, kdim, two_f = w.shape
    tk = 512
    src = jnp.arange(2 * LANES, dtype=jnp.int32)[:, None]
    dst = jnp.arange(2 * LANES, dtype=jnp.int32)[None, :]
    perm = jnp.where(dst < LANES, src == 2 * dst, src == 2 * (dst - LANES) + 1).astype(BF16)
    return pl.pallas_call(
        _regroup_kernel,
        out_shape=jax.ShapeDtypeStruct((n, kdim, two_f), BF16),
        grid=(n, kdim // tk),
        in_specs=[
            pl.BlockSpec((1, tk, two_f), lambda e, j: (e, j, 0)),
            pl.BlockSpec((2 * LANES, 2 * LANES), lambda e, j: (0, 0)),
        ],
        out_specs=pl.BlockSpec((1, tk, two_f), lambda e, j: (e, j, 0)),
        compiler_params=_cparams("arbitrary", "arbitrary"),
        name="regroup_up_weights",
    )(w, perm)


def _expert_kernel(blk_e_ref, nused_ref, xs_ref, wu_ref, bu_ref, wd_ref, bd_ref, ys_ref, *, d_exp):
    del blk_e_ref
    used = pl.program_id(0) < nused_ref[0]

    @pl.when(jnp.logical_not(used))
    def _():
        ys_ref[...] = jnp.zeros_like(ys_ref)

    @pl.when(used)
    def _():
        x = xs_ref[...].astype(BF16)
        hu = _dot(x, wu_ref[0]) + bu_ref[0]
        gate = jnp.minimum(hu[:, :d_exp], SWIGLU_LIMIT)
        lin = jnp.clip(hu[:, d_exp:], -SWIGLU_LIMIT, SWIGLU_LIMIT)
        act = gate * _sigmoid(SWIGLU_ALPHA * gate) * (lin + 1.0)
        ys_ref[...] = _dot(act.astype(BF16), wd_ref[0]) + bd_ref[0]


def _expert_ffn(xs, blk_e, nused, w_up, b_up, w_down, b_down):
    cap, d = xs.shape
    n_exp, _, two_de = w_up.shape
    d_exp = two_de // 2
    tm = MOE_ROW_TILE
    nblk = cap // tm
    rows = lambda i, be, nu: (jnp.minimum(i, nu[0] - 1), 0)
    by_e = lambda i, be, nu: (be[i], 0, 0)
    return pl.pallas_call(
        functools.partial(_expert_kernel, d_exp=d_exp),
        out_shape=jax.ShapeDtypeStruct((cap, d), F32),
        grid_spec=pltpu.PrefetchScalarGridSpec(
            num_scalar_prefetch=2,
            grid=(nblk,),
            in_specs=[
                pl.BlockSpec((tm, d), rows),
                pl.BlockSpec((1, d, two_de), by_e),
                pl.BlockSpec((1, 1, two_de), by_e),
                pl.BlockSpec((1, d_exp, d), by_e),
                pl.BlockSpec((1, 1, d), by_e),
            ],
            out_specs=pl.BlockSpec((tm, d), lambda i, be, nu: (i, 0)),
        ),
        compiler_params=_cparams("arbitrary"),
        name="moe_experts",
    )(blk_e, nused, xs, w_up, b_up, w_down, b_down)


def _combine_kernel(dest_ref, x_ref, mod_ref, meta_ref, ys_hbm, png_ref, pnb_ref, o_ref, buf, sem, *,
                    alpha):
    tm = x_ref.shape[0]

    def body(r, carry):
        for kk in range(TOP_K):
            d = dest_ref[0, 0, r * TOP_K + kk]
            pltpu.make_async_copy(ys_hbm.at[pl.ds(d, 1)], buf.at[kk, pl.ds(r, 1)], sem).start()
        return carry

    lax.fori_loop(0, tm, body, 0)
    for kk in range(TOP_K):
        pltpu.make_async_copy(ys_hbm.at[pl.ds(0, tm)], buf.at[kk], sem).wait()
    meta = meta_ref[...]
    y = meta[:, TOP_K:TOP_K + 1] * buf[0]
    for kk in range(1, TOP_K):
        y = y + meta[:, TOP_K + kk:TOP_K + kk + 1] * buf[kk]
    x = x_ref[...]
    gate = mod_ref[0][5:6]
    o_ref[...] = _layer_norm(alpha * x + (1.0 + gate) * y, png_ref[...], pnb_ref[...])


def _combine(x1, mod, seq, alpha, meta, dest, ys, png, pnb):
    t, d = x1.shape
    tm = DMA_ROW_TILE
    row = lambda i: (i, 0)
    full2 = lambda i: (0, 0)
    return pl.pallas_call(
        functools.partial(_combine_kernel, alpha=alpha),
        out_shape=jax.ShapeDtypeStruct((t, d), F32),
        grid=(t // tm,),
        in_specs=[
            pl.BlockSpec((1, 1, tm * TOP_K), lambda i: (i, 0, 0), memory_space=pltpu.SMEM),
            pl.BlockSpec((tm, d), row),
            pl.BlockSpec((1, 6, d), lambda i: ((i * tm) // seq, 0, 0)),
            pl.BlockSpec((tm, LANES), row),
            pl.BlockSpec(memory_space=pl.ANY),
            pl.BlockSpec((1, d), full2),
            pl.BlockSpec((1, d), full2),
        ],
        out_specs=pl.BlockSpec((tm, d), row),
        scratch_shapes=[pltpu.VMEM((TOP_K, tm, d), F32), pltpu.SemaphoreType.DMA(())],
        compiler_params=_cparams("arbitrary"),
        name="moe_combine",
    )(dest.reshape(t // tm, 1, tm * TOP_K), x1, mod, meta, ys, png.reshape(1, -1), pnb.reshape(1, -1))


def _moe_layer(x1, mod, seq, alpha, w_router, b_router, w_up, b_up, w_down, b_down, png, pnb):
    t, d = x1.shape
    n_exp = w_router.shape[1]
    tm = MOE_ROW_TILE
    h, meta, cnt = _route(x1, mod, seq, w_router, b_router)

    counts = cnt[0, :n_exp].astype(jnp.int32)
    padded = (counts + tm - 1) // tm * tm
    pad_end = jnp.cumsum(padded)
    pad_start = pad_end - padded
    idx = meta[:, :TOP_K].astype(jnp.int32)
    rank = meta[:, 2 * TOP_K:3 * TOP_K].astype(jnp.int32)
    dest = pad_start[idx] + rank
    nblk = -(-(t * TOP_K + n_exp * tm) // tm)
    blk_start = jnp.arange(nblk, dtype=jnp.int32) * tm
    blk_e = jnp.minimum(jnp.sum(blk_start[:, None] >= pad_end[None, :], axis=1), n_exp - 1).astype(jnp.int32)
    nused = (pad_end[-1:] // tm).astype(jnp.int32)

    xs = _scatter_rows(h, dest, nblk * tm)
    wu = _regroup_even_odd(w_up)
    bu = jnp.concatenate([b_up[:, 0::2], b_up[:, 1::2]], axis=-1).reshape(n_exp, 1, -1)
    ys = _expert_ffn(xs, blk_e, nused, wu, bu, w_down.astype(BF16), b_down.reshape(n_exp, 1, -1))
    return _combine(x1, mod, seq, alpha, meta, dest, ys, png, pnb)


def kernel(x, c, ada_w, ada_b, ln_g, ln_b, gla_w_in, gla_wg2_f, gla_bg_f, gla_wg2_b, gla_bg_b, gla_norm_g, gla_w_out, sgu_w_in, sgu_b_in, sgu_ln_g, sgu_ln_b, sgu_w_s, sgu_b_s, sgu_w_out, sgu_b_out, moe_w_router, moe_b_router, moe_w_up, moe_b_up, moe_w_down, moe_b_down):
    bsz, seq, d = x.shape
    depth = ada_w.shape[0]
    alpha = (2.0 * depth) ** 0.25
    mod_all = _adaln(c, ada_w, ada_b).reshape(depth, bsz, 6, d)
    xt = x.reshape(bsz * seq, d)
    for layer in range(depth):
        mod = mod_all[layer]
        i = layer // 2
        if layer % 2 == 0:
            x1 = _gla_layer(xt, mod, bsz, seq, alpha, gla_w_in[i], gla_wg2_f[i], gla_bg_f[i],
                            gla_wg2_b[i], gla_bg_b[i], gla_norm_g[i], gla_w_out[i],
                            ln_g[layer, 0], ln_b[layer, 0])
        else:
            x1 = _sgu_layer(xt, mod, seq, alpha, sgu_w_in[i], sgu_b_in[i], sgu_ln_g[i], sgu_ln_b[i],
                            sgu_w_s[i], sgu_b_s[i], sgu_w_out[i], sgu_b_out[i],
                            ln_g[layer, 0], ln_b[layer, 0])
        xt = _moe_layer(x1, mod, seq, alpha, moe_w_router[layer], moe_b_router[layer],
                        moe_w_up[layer], moe_b_up[layer], moe_w_down[layer], moe_b_down[layer],
                        ln_g[layer, 1], ln_b[layer, 1])
    return xt.reshape(bsz, seq, d)
```

```python
import functools
import math

import jax
import jax.numpy as jnp
from jax import lax
from jax.experimental import pallas as pl
from jax.experimental.pallas import tpu as pltpu

F32 = jnp.float32
BF16 = jnp.bfloat16

LN_EPS = 1e-5
GLA_HEADS = 4
GLA_GATE_RANK = 16
GLA_GATE_TAU = 16.0
SGU_CHUNK = 128
TOP_K = 4
SWIGLU_LIMIT = 7.0
SWIGLU_ALPHA = 1.702

LANES = 128
SUBLANES = 8
VMEM_LIMIT_BYTES = 56 * 1024 * 1024

ROW_TILE = 256
MOE_ROW_TILE = 512
EXPERT_HIDDEN_CHUNK = 1024
DMA_ROW_TILE = 256
SCAN_CHUNK = 64
SCAN_GROUP = SUBLANES
SCAN_BLOCK = 512


def _cparams(*sem):
    return pltpu.CompilerParams(dimension_semantics=sem, vmem_limit_bytes=VMEM_LIMIT_BYTES)


def _split_bf16(a):
    hi = a.astype(BF16)
    lo = (a - hi.astype(F32)).astype(BF16)
    return hi, lo


def _dot(a, b):
    return jnp.dot(a, b, preferred_element_type=F32)


def _dot3(a, w):
    a_hi, a_lo = _split_bf16(a)
    w_hi, w_lo = _split_bf16(w)
    return _dot(a_hi, w_hi) + _dot(a_hi, w_lo) + _dot(a_lo, w_hi)


def _layer_norm(x, g, b):
    mu = jnp.mean(x, axis=-1, keepdims=True)
    xc = x - mu
    var = jnp.mean(xc * xc, axis=-1, keepdims=True)
    return xc * lax.rsqrt(var + LN_EPS) * g + b


def _sigmoid(x):
    return 1.0 / (1.0 + jnp.exp(-x))


def _adaln_kernel(c_ref, w_ref, b_ref, o_ref):
    c = c_ref[...]
    sc = c * _sigmoid(c)
    o_ref[0] = _dot3(sc, w_ref[0]) + b_ref[0]


def _adaln(c, ada_w, ada_b):
    depth, d, n = ada_w.shape
    bsz = c.shape[0]
    tn = 2048
    return pl.pallas_call(
        _adaln_kernel,
        out_shape=jax.ShapeDtypeStruct((depth, bsz, n), F32),
        grid=(depth, n // tn),
        in_specs=[
            pl.BlockSpec((bsz, d), lambda l, j: (0, 0)),
            pl.BlockSpec((1, d, tn), lambda l, j: (l, 0, j)),
            pl.BlockSpec((1, 1, tn), lambda l, j: (l, 0, j)),
        ],
        out_specs=pl.BlockSpec((1, bsz, tn), lambda l, j: (l, 0, j)),
        compiler_params=_cparams("arbitrary", "arbitrary"),
        name="adaln",
    )(c, ada_w, ada_b.reshape(depth, 1, n))


def _sgu_kernel(x_ref, mod_ref, w_in_ref, b_in_ref, vg_ref, vb_ref, ws_ref, bst_ref, w_out_ref,
                b_out_ref, png_ref, pnb_ref, o_ref, gated_ref, *, alpha, width, groups):
    x = x_ref[...]
    mod = mod_ref[0]
    shift, scale, gate = mod[0:1], mod[1:2], mod[2:3]
    h = (x * (1.0 + scale) + shift).astype(BF16)
    z = _dot(h, w_in_ref[...]) + b_in_ref[...]
    z = 0.5 * z * (1.0 + lax.erf(z * math.sqrt(0.5)))
    u = z[:, :width]
    v = _layer_norm(z[:, width:], vg_ref[...], vb_ref[...]).astype(BF16)
    gd = width // groups
    for n in range(x.shape[0] // SGU_CHUNK):
        rows = slice(n * SGU_CHUNK, (n + 1) * SGU_CHUNK)
        for g in range(groups):
            cols = slice(g * gd, (g + 1) * gd)
            sv = _dot(ws_ref[g], v[rows, cols]) + bst_ref[:, g:g + 1]
            gated_ref[rows, cols] = (u[rows, cols] * sv).astype(BF16)
    y = _dot(gated_ref[...], w_out_ref[...]) + b_out_ref[...]
    o_ref[...] = _layer_norm(alpha * x + (1.0 + gate) * y, png_ref[...], pnb_ref[...])


def _sgu_layer(xt, mod, seq, alpha, w_in, b_in, vg, vb, w_s, b_s, w_out, b_out, png, pnb):
    t, d = xt.shape
    width = w_out.shape[0]
    groups = w_s.shape[0]
    tm = ROW_TILE
    row = lambda i: (i, 0)
    full2 = lambda i: (0, 0)
    return pl.pallas_call(
        functools.partial(_sgu_kernel, alpha=alpha, width=width, groups=groups),
        out_shape=jax.ShapeDtypeStruct((t, d), F32),
        grid=(t // tm,),
        in_specs=[
            pl.BlockSpec((tm, d), row),
            pl.BlockSpec((1, 6, d), lambda i: ((i * tm) // seq, 0, 0)),
            pl.BlockSpec((d, 2 * width), full2),
            pl.BlockSpec((1, 2 * width), full2),
            pl.BlockSpec((1, width), full2),
            pl.BlockSpec((1, width), full2),
            pl.BlockSpec((groups, SGU_CHUNK, SGU_CHUNK), lambda i: (0, 0, 0)),
            pl.BlockSpec((SGU_CHUNK, groups), full2),
            pl.BlockSpec((width, d), full2),
            pl.BlockSpec((1, d), full2),
            pl.BlockSpec((1, d), full2),
            pl.BlockSpec((1, d), full2),
        ],
        out_specs=pl.BlockSpec((tm, d), row),
        scratch_shapes=[pltpu.VMEM((tm, width), BF16)],
        compiler_params=_cparams("arbitrary"),
        name="sgu_mixer",
    )(xt, mod, w_in.astype(BF16), b_in.reshape(1, -1), vg.reshape(1, -1), vb.reshape(1, -1),
      w_s.astype(BF16), b_s.T, w_out.astype(BF16), b_out.reshape(1, -1), png.reshape(1, -1),
      pnb.reshape(1, -1))


def _gla_in_kernel(x_ref, mod_ref, w_ref, wlr_ref, w2_ref, bg_ref, q_ref, k_ref, v_ref, r_ref,
                   gf_ref, gb_ref, *, qk, vo, q_scale):
    x = x_ref[...]
    mod = mod_ref[0]
    shift, scale = mod[0:1], mod[1:2]
    h = (x * (1.0 + scale) + shift).astype(BF16)
    p = _dot(h, w_ref[...])
    q_ref[...] = p[:, :qk] * q_scale
    k_ref[...] = p[:, qk:2 * qk]
    v_ref[...] = p[:, 2 * qk:2 * qk + vo].astype(BF16)
    r_ref[...] = p[:, 2 * qk + vo:]
    lr = _dot(h, wlr_ref[...])
    z = _dot(lr.astype(BF16), w2_ref[...]) + bg_ref[...]
    g = (jnp.minimum(z, 0.0) - jnp.log1p(jnp.exp(-jnp.abs(z)))) * (1.0 / GLA_GATE_TAU)
    gf_ref[...] = g[:, :qk]
    gb_ref[...] = g[:, qk:]


def _gla_in(xt, mod, seq, w_main, w_lr, w2, bg, qk, vo):
    t, d = xt.shape
    tm = ROW_TILE
    row = lambda i: (i, 0)
    full2 = lambda i: (0, 0)
    dk = qk // GLA_HEADS
    return pl.pallas_call(
        functools.partial(_gla_in_kernel, qk=qk, vo=vo, q_scale=dk ** -0.5),
        out_shape=(
            jax.ShapeDtypeStruct((t, qk), F32),
            jax.ShapeDtypeStruct((t, qk), F32),
            jax.ShapeDtypeStruct((t, vo), BF16),
            jax.ShapeDtypeStruct((t, vo), F32),
            jax.ShapeDtypeStruct((t, qk), F32),
            jax.ShapeDtypeStruct((t, qk), F32),
        ),
        grid=(t // tm,),
        in_specs=[
            pl.BlockSpec((tm, d), row),
            pl.BlockSpec((1, 6, d), lambda i: ((i * tm) // seq, 0, 0)),
            pl.BlockSpec(w_main.shape, full2),
            pl.BlockSpec(w_lr.shape, full2),
            pl.BlockSpec(w2.shape, full2),
            pl.BlockSpec(bg.shape, full2),
        ],
        out_specs=(
            pl.BlockSpec((tm, qk), row),
            pl.BlockSpec((tm, qk), row),
            pl.BlockSpec((tm, vo), row),
            pl.BlockSpec((tm, vo), row),
            pl.BlockSpec((tm, qk), row),
            pl.BlockSpec((tm, qk), row),
        ),
        compiler_params=_cparams("arbitrary"),
        name="gla_in",
    )(xt, mod, w_main, w_lr, w2, bg)


def _scan_consts(reverse):
    c_len, grp = SCAN_CHUNK, SCAN_GROUP
    ngrp = c_len // grp
    row = lax.broadcasted_iota(jnp.int32, (c_len, c_len), 0)
    col = lax.broadcasted_iota(jnp.int32, (c_len, c_len), 1)
    later = (col >= row) if reverse else (col <= row)
    grp_r, grp_c = row // grp, col // grp
    gi = lax.broadcasted_iota(jnp.int32, (ngrp, grp, c_len), 0)
    ii = lax.broadcasted_iota(jnp.int32, (ngrp, grp, c_len), 1)
    lane = lax.broadcasted_iota(jnp.int32, (ngrp, grp, c_len), 2)
    diag_masks = []
    for jj in range(grp):
        tri_ok = (ii <= jj) if reverse else (ii >= jj)
        diag_masks.append((lane == gi * grp + jj) & tri_ok)
    return dict(
        tri=jnp.where(later, 1.0, 0.0).astype(BF16),
        off_mask=(grp_r < grp_c) if reverse else (grp_r > grp_c),
        row_grp=lax.broadcasted_iota(jnp.int32, (c_len, 1), 0) // grp,
        diag_masks=diag_masks,
        edge_in_grp=0 if reverse else grp - 1,
        edge_in_chunk=0 if reverse else c_len - 1,
    )


def _scan_chunk_head(cst, q, k, v, g, s_t):
    c_len, grp = SCAN_CHUNK, SCAN_GROUP
    ngrp = c_len // grp
    dk = q.shape[1]
    nt_dims = (((1,), (1,)), ((), ()))
    tn_dims = (((0,), (0,)), ((), ()))
    g_hi, g_lo = _split_bf16(g)
    b = _dot(cst["tri"], g_hi) + _dot(cst["tri"], g_lo)
    b3 = b.reshape(ngrp, grp, dk)
    q3 = q.reshape(ngrp, grp, dk)
    k3 = k.reshape(ngrp, grp, dk)
    eg = cst["edge_in_grp"]
    e_all = b3[:, eg:eg + 1, :]
    e_own = jnp.broadcast_to(e_all, b3.shape).reshape(c_len, dk)
    ec = cst["edge_in_chunk"]
    b_edge = b[ec:ec + 1, :]

    qi = (q * jnp.exp(b)).astype(BF16)
    o = lax.dot_general(qi, s_t.astype(BF16), nt_dims, preferred_element_type=F32)

    ktil = k * jnp.exp(e_own - b)
    qhat, khat = [], []
    for j in range(ngrp):
        e_j = e_all[j]
        qhat.append((q * jnp.exp(jnp.minimum(b - e_j, 0.0))).astype(BF16))
        khat.append(jnp.where(cst["row_grp"] == j, ktil, 0.0).astype(BF16))
    a_off = lax.dot_general(jnp.concatenate(qhat, axis=1), jnp.concatenate(khat, axis=1),
                            nt_dims, preferred_element_type=F32)
    a3 = jnp.where(cst["off_mask"], a_off, 0.0).reshape(ngrp, grp, c_len)

    for jj in range(grp):
        kb = k3[:, jj:jj + 1, :]
        bb = b3[:, jj:jj + 1, :]
        t = q3 * kb * jnp.exp(jnp.minimum(b3 - bb, 0.0))
        s = jnp.sum(t, axis=-1, keepdims=True)
        a3 = jnp.where(cst["diag_masks"][jj], s, a3)
    a = a3.reshape(c_len, c_len).astype(BF16)
    o = o + _dot(a, v)

    kst = (k * jnp.exp(b_edge - b)).astype(BF16)
    upd = lax.dot_general(v, kst, tn_dims, preferred_element_type=F32)
    return o, s_t * jnp.exp(b_edge) + upd


def _gla_scan_kernel(qf_ref, kf_ref, vf_ref, gf_ref, qb_ref, kb_ref, vb_ref, gb_ref, of_ref, ob_ref,
                     st_ref, *, heads, dk, dv):
    c_len = SCAN_CHUNK
    nchunk = qf_ref.shape[0] // c_len

    @pl.when(pl.program_id(1) == 0)
    def _():
        st_ref[...] = jnp.zeros_like(st_ref)

    cst_f = _scan_consts(False)
    cst_b = _scan_consts(True)

    def chunk_body(ci, carry):
        rows_f = pl.ds(pl.multiple_of(ci * c_len, c_len), c_len)
        rows_b = pl.ds(pl.multiple_of((nchunk - 1 - ci) * c_len, c_len), c_len)
        for hd in range(heads):
            kcols = slice(hd * dk, (hd + 1) * dk)
            vcols = slice(hd * dv, (hd + 1) * dv)
            o, s_new = _scan_chunk_head(cst_f, qf_ref[rows_f, kcols], kf_ref[rows_f, kcols],
                                        vf_ref[rows_f, vcols], gf_ref[rows_f, kcols], st_ref[0, hd])
            of_ref[rows_f, vcols] = o
            st_ref[0, hd] = s_new
            o, s_new = _scan_chunk_head(cst_b, qb_ref[rows_b, kcols], kb_ref[rows_b, kcols],
                                        vb_ref[rows_b, vcols], gb_ref[rows_b, kcols], st_ref[1, hd])
            ob_ref[rows_b, vcols] = o
            st_ref[1, hd] = s_new
        return carry

    lax.fori_loop(0, nchunk, chunk_body, 0)


def _gla_scan(q, k, v, gf, gb, bsz, seq):
    t, qk = q.shape
    vo = v.shape[1]
    dk, dv = qk // GLA_HEADS, vo // GLA_HEADS
    sb = min(SCAN_BLOCK, seq)
    nsb = seq // sb
    fwd = lambda b, s: (b * nsb + s, 0)
    bwd = lambda b, s: (b * nsb + (nsb - 1 - s), 0)
    return pl.pallas_call(
        functools.partial(_gla_scan_kernel, heads=GLA_HEADS, dk=dk, dv=dv),
        out_shape=(jax.ShapeDtypeStruct((t, vo), F32), jax.ShapeDtypeStruct((t, vo), F32)),
        grid=(bsz, nsb),
        in_specs=[
            pl.BlockSpec((sb, qk), fwd),
            pl.BlockSpec((sb, qk), fwd),
            pl.BlockSpec((sb, vo), fwd),
            pl.BlockSpec((sb, qk), fwd),
            pl.BlockSpec((sb, qk), bwd),
            pl.BlockSpec((sb, qk), bwd),
            pl.BlockSpec((sb, vo), bwd),
            pl.BlockSpec((sb, qk), bwd),
        ],
        out_specs=(pl.BlockSpec((sb, vo), fwd), pl.BlockSpec((sb, vo), bwd)),
        scratch_shapes=[pltpu.VMEM((2, GLA_HEADS, dv, dk), F32)],
        compiler_params=_cparams("arbitrary", "arbitrary"),
        name="gla_scan",
    )(q, k, v, gf, q, k, v, gb)


def _gla_out_kernel(x_ref, mod_ref, of_ref, ob_ref, r_ref, ng_ref, w_out_ref, png_ref, pnb_ref,
                    o_ref, *, alpha, heads):
    x = x_ref[...]
    gate = mod_ref[0][2:3]
    o = of_ref[...] + ob_ref[...]
    dv = o.shape[1] // heads
    pieces = []
    for hd in range(heads):
        oh = o[:, hd * dv:(hd + 1) * dv]
        ms = jnp.mean(oh * oh, axis=-1, keepdims=True)
        pieces.append(oh * lax.rsqrt(ms + LN_EPS) * ng_ref[...])
    on = jnp.concatenate(pieces, axis=1)
    r = r_ref[...]
    gated = (on * (r * _sigmoid(r))).astype(BF16)
    y = _dot(gated, w_out_ref[...])
    o_ref[...] = _layer_norm(alpha * x + (1.0 + gate) * y, png_ref[...], pnb_ref[...])


def _gla_out(xt, mod, seq, alpha, o_f, o_b, r, norm_g, w_out, png, pnb):
    t, d = xt.shape
    vo = o_f.shape[1]
    tm = ROW_TILE
    row = lambda i: (i, 0)
    full2 = lambda i: (0, 0)
    return pl.pallas_call(
        functools.partial(_gla_out_kernel, alpha=alpha, heads=GLA_HEADS),
        out_shape=jax.ShapeDtypeStruct((t, d), F32),
        grid=(t // tm,),
        in_specs=[
            pl.BlockSpec((tm, d), row),
            pl.BlockSpec((1, 6, d), lambda i: ((i * tm) // seq, 0, 0)),
            pl.BlockSpec((tm, vo), row),
            pl.BlockSpec((tm, vo), row),
            pl.BlockSpec((tm, vo), row),
            pl.BlockSpec((1, vo // GLA_HEADS), full2),
            pl.BlockSpec((vo, d), full2),
            pl.BlockSpec((1, d), full2),
            pl.BlockSpec((1, d), full2),
        ],
        out_specs=pl.BlockSpec((tm, d), row),
        compiler_params=_cparams("arbitrary"),
        name="gla_out",
    )(xt, mod, o_f, o_b, r, norm_g.reshape(1, -1), w_out.astype(BF16), png.reshape(1, -1),
      pnb.reshape(1, -1))


def _gla_layer(xt, mod, bsz, seq, alpha, w_in, wg2_f, bg_f, wg2_b, bg_b, norm_g, w_out, png, pnb):
    rank = GLA_GATE_RANK
    qk = wg2_f.shape[1]
    vo = w_out.shape[0]
    n_main = 2 * qk + 2 * vo
    w_main = w_in[:, :n_main].astype(BF16)
    w_lr = w_in[:, n_main:].astype(BF16)
    zeros = jnp.zeros((rank, qk), F32)
    w2 = jnp.concatenate([jnp.concatenate([wg2_f, zeros], axis=1),
                          jnp.concatenate([zeros, wg2_b], axis=1)], axis=0).astype(BF16)
    bg = jnp.concatenate([bg_f, bg_b]).reshape(1, -1)
    q, k, v, r, gf, gb = _gla_in(xt, mod, seq, w_main, w_lr, w2, bg, qk, vo)
    o_f, o_b = _gla_scan(q, k, v, gf, gb, bsz, seq)
    return _gla_out(xt, mod, seq, alpha, o_f, o_b, r, norm_g, w_out, png, pnb)


def _route_kernel(x_ref, mod_ref, wr_ref, br_ref, h_ref, meta_ref, cnt_ref, carry_ref, *, n_exp):
    i = pl.program_id(0)

    @pl.when(i == 0)
    def _():
        carry_ref[...] = jnp.zeros_like(carry_ref)

    x = x_ref[...]
    mod = mod_ref[0]
    shift, scale = mod[3:4], mod[4:5]
    h = x * (1.0 + scale) + shift
    h_ref[...] = h
    tm = x.shape[0]
    lane = lax.broadcasted_iota(jnp.int32, (tm, LANES), 1)
    logits = _dot3(h, wr_ref[...]) + br_ref[...]
    neg_inf = jnp.float32(-jnp.inf)
    cur = jnp.where(lane < n_exp, logits, neg_inf)
    vals, idxs, hots = [], [], []
    for _ in range(TOP_K):
        m = jnp.max(cur, axis=-1, keepdims=True)
        idx = jnp.min(jnp.where(cur == m, lane, LANES), axis=-1, keepdims=True)
        hot = lane == idx
        cur = jnp.where(hot, neg_inf, cur)
        vals.append(m)
        idxs.append(idx)
        hots.append(hot)
    ps = [jnp.exp(vk - vals[0]) for vk in vals]
    den = ps[0] + ps[1] + ps[2] + ps[3]
    ws = [p / den for p in ps]

    cnt = jnp.zeros((tm, LANES), F32)
    for hot in hots:
        cnt = cnt + jnp.where(hot, 1.0, 0.0)
    r_i = lax.broadcasted_iota(jnp.int32, (tm, tm), 0)
    c_i = lax.broadcasted_iota(jnp.int32, (tm, tm), 1)
    strict = jnp.where(c_i < r_i, 1.0, 0.0).astype(BF16)
    carry = carry_ref[0:1, :]
    before = _dot(strict, cnt.astype(BF16)) + carry
    meta = jnp.zeros((tm, LANES), F32)
    for kk in range(TOP_K):
        rank = jnp.sum(jnp.where(hots[kk], before, 0.0), axis=-1, keepdims=True)
        meta = jnp.where(lane == kk, idxs[kk].astype(F32), meta)
        meta = jnp.where(lane == TOP_K + kk, ws[kk], meta)
        meta = jnp.where(lane == 2 * TOP_K + kk, rank, meta)
    meta_ref[...] = meta
    new_carry = carry + jnp.sum(cnt, axis=0, keepdims=True)
    carry_ref[...] = jnp.broadcast_to(new_carry, carry_ref.shape)
    cnt_ref[...] = jnp.broadcast_to(new_carry, cnt_ref.shape)


def _route(x1, mod, seq, w_router, b_router):
    t, d = x1.shape
    n_exp = w_router.shape[1]
    tm = ROW_TILE
    wr = jnp.pad(w_router, ((0, 0), (0, LANES - n_exp)))
    br = jnp.pad(b_router, (0, LANES - n_exp)).reshape(1, LANES)
    row = lambda i: (i, 0)
    full2 = lambda i: (0, 0)
    return pl.pallas_call(
        functools.partial(_route_kernel, n_exp=n_exp),
        out_shape=(
            jax.ShapeDtypeStruct((t, d), F32),
            jax.ShapeDtypeStruct((t, LANES), F32),
            jax.ShapeDtypeStruct((SUBLANES, LANES), F32),
        ),
        grid=(t // tm,),
        in_specs=[
            pl.BlockSpec((tm, d), row),
            pl.BlockSpec((1, 6, d), lambda i: ((i * tm) // seq, 0, 0)),
            pl.BlockSpec((d, LANES), full2),
            pl.BlockSpec((1, LANES), full2),
        ],
        out_specs=(
            pl.BlockSpec((tm, d), row),
            pl.BlockSpec((tm, LANES), row),
            pl.BlockSpec((SUBLANES, LANES), full2),
        ),
        scratch_shapes=[pltpu.VMEM((SUBLANES, LANES), F32)],
        compiler_params=_cparams("arbitrary"),
        name="moe_route",
    )(x1, mod, wr, br)


def _scatter_kernel(pad_end_ref, nused_ref, dest_ref, h_ref, xs_hbm, zbuf, sem, zsem, *, n_exp, nblk):
    tm = h_ref.shape[0]
    tz = zbuf.shape[0]

    @pl.when(pl.program_id(0) == 0)
    def _():
        zbuf[...] = jnp.zeros_like(zbuf)

        def expert_tails(fn):
            for e in range(n_exp):
                end = pad_end_ref[e]
                prev = pad_end_ref[e - 1] if e else 0

                @pl.when(end > prev)
                def _():
                    off = pl.multiple_of(end - tz, tz)
                    fn(pltpu.make_async_copy(zbuf, xs_hbm.at[pl.ds(off, tz)], zsem))

        def unused_blocks(fn):
            def one(b, carry):
                off = pl.multiple_of(b * tz, tz)
                fn(pltpu.make_async_copy(zbuf, xs_hbm.at[pl.ds(off, tz)], zsem))
                return carry
            lax.fori_loop(nused_ref[0], nblk, one, 0)

        expert_tails(lambda cp: cp.start())
        unused_blocks(lambda cp: cp.start())
        expert_tails(lambda cp: cp.wait())
        unused_blocks(lambda cp: cp.wait())

    def body(r, carry):
        for kk in range(TOP_K):
            d = dest_ref[0, 0, r * TOP_K + kk]
            pltpu.make_async_copy(h_ref.at[pl.ds(r, 1)], xs_hbm.at[pl.ds(d, 1)], sem).start(priority=kk % 2)
        return carry

    lax.fori_loop(0, tm, body, 0)
    for _ in range(TOP_K):
        pltpu.make_async_copy(h_ref, xs_hbm.at[pl.ds(0, tm)], sem).wait()


def _scatter_rows(h, dest, pad_end, nused, cap):
    t, d = h.shape
    tm = DMA_ROW_TILE
    tz = MOE_ROW_TILE
    n_exp = pad_end.shape[0]
    return pl.pallas_call(
        functools.partial(_scatter_kernel, n_exp=n_exp, nblk=cap // tz),
        out_shape=jax.ShapeDtypeStruct((cap, d), F32),
        grid_spec=pltpu.PrefetchScalarGridSpec(
            num_scalar_prefetch=2,
            grid=(t // tm,),
            in_specs=[
                pl.BlockSpec((1, 1, tm * TOP_K), lambda i, pe, nu: (i, 0, 0), memory_space=pltpu.SMEM),
                pl.BlockSpec((tm, d), lambda i, pe, nu: (i, 0)),
            ],
            out_specs=pl.BlockSpec(memory_space=pl.ANY),
            scratch_shapes=[pltpu.VMEM((tz, d), F32), pltpu.SemaphoreType.DMA(()),
                            pltpu.SemaphoreType.DMA(())],
        ),
        compiler_params=_cparams("arbitrary"),
        name="moe_scatter",
    )(pad_end, nused, dest.reshape(t // tm, 1, tm * TOP_K), h)


def _regroup_kernel(w_ref, p_ref, o_ref):
    half = w_ref.shape[2] // 2
    for cidx in range(w_ref.shape[2] // (2 * LANES)):
        blk = w_ref[0, :, cidx * 2 * LANES:(cidx + 1) * 2 * LANES].astype(BF16)
        t = _dot(blk, p_ref[...]).astype(BF16)
        o_ref[0, :, cidx * LANES:(cidx + 1) * LANES] = t[:, :LANES]
        o_ref[0, :, half + cidx * LANES:half + (cidx + 1) * LANES] = t[:, LANES:]


def _regroup_even_odd(w):
    n, kdim, two_f = w.shape
    tk = 512
    src = jnp.arange(2 * LANES, dtype=jnp.int32)[:, None]
    dst = jnp.arange(2 * LANES, dtype=jnp.int32)[None, :]
    perm = jnp.where(dst < LANES, src == 2 * dst, src == 2 * (dst - LANES) + 1).astype(BF16)
    return pl.pallas_call(
        _regroup_kernel,
        out_shape=jax.ShapeDtypeStruct((n, kdim, two_f), BF16),
        grid=(n, kdim // tk),
        in_specs=[
            pl.BlockSpec((1, tk, two_f), lambda e, j: (e, j, 0)),
            pl.BlockSpec((2 * LANES, 2 * LANES), lambda e, j: (0, 0)),
        ],
        out_specs=pl.BlockSpec((1, tk, two_f), lambda e, j: (e, j, 0)),
        compiler_params=_cparams("arbitrary", "arbitrary"),
        name="regroup_up_weights",
    )(w, perm)


def _expert_kernel(blk_e_ref, nused_ref, xs_ref, wu_ref, bu_ref, wd_ref, bd_ref, ys_ref, *, d_exp):
    del blk_e_ref
    used = pl.program_id(0) < nused_ref[0]

    @pl.when(jnp.logical_not(used))
    def _():
        ys_ref[...] = jnp.zeros_like(ys_ref)

    @pl.when(used)
    def _():
        x = xs_ref[...].astype(BF16)
        acc = None
        for c0 in range(0, d_exp, EXPERT_HIDDEN_CHUNK):
            gcols = slice(c0, c0 + EXPERT_HIDDEN_CHUNK)
            lcols = slice(d_exp + c0, d_exp + c0 + EXPERT_HIDDEN_CHUNK)
            gate = _dot(x, wu_ref[0, :, gcols]) + bu_ref[0, :, gcols]
            lin = _dot(x, wu_ref[0, :, lcols]) + bu_ref[0, :, lcols]
            gate = jnp.minimum(gate, SWIGLU_LIMIT)
            lin = jnp.clip(lin, -SWIGLU_LIMIT, SWIGLU_LIMIT)
            act = gate * _sigmoid(SWIGLU_ALPHA * gate) * (lin + 1.0)
            part = _dot(act.astype(BF16), wd_ref[0, gcols, :])
            acc = part if acc is None else acc + part
        ys_ref[...] = acc + bd_ref[0]


def _expert_ffn(xs, blk_e, nused, w_up, b_up, w_down, b_down):
    cap, d = xs.shape
    n_exp, _, two_de = w_up.shape
    d_exp = two_de // 2
    tm = MOE_ROW_TILE
    nblk = cap // tm
    rows = lambda i, be, nu: (jnp.minimum(i, nu[0] - 1), 0)
    by_e = lambda i, be, nu: (be[i], 0, 0)
    return pl.pallas_call(
        functools.partial(_expert_kernel, d_exp=d_exp),
        out_shape=jax.ShapeDtypeStruct((cap, d), F32),
        grid_spec=pltpu.PrefetchScalarGridSpec(
            num_scalar_prefetch=2,
            grid=(nblk,),
            in_specs=[
                pl.BlockSpec((tm, d), rows),
                pl.BlockSpec((1, d, two_de), by_e),
                pl.BlockSpec((1, 1, two_de), by_e),
                pl.BlockSpec((1, d_exp, d), by_e),
                pl.BlockSpec((1, 1, d), by_e),
            ],
            out_specs=pl.BlockSpec((tm, d), lambda i, be, nu: (i, 0)),
        ),
        compiler_params=_cparams("arbitrary"),
        name="moe_experts",
    )(blk_e, nused, xs, w_up, b_up, w_down, b_down)


def _combine_kernel(dest_ref, x_ref, mod_ref, meta_ref, ys_hbm, png_ref, pnb_ref, o_ref, buf, sem, *,
                    alpha):
    tm = x_ref.shape[0]

    def body(r, carry):
        for kk in range(TOP_K):
            d = dest_ref[0, 0, r * TOP_K + kk]
            pltpu.make_async_copy(ys_hbm.at[pl.ds(d, 1)], buf.at[kk, pl.ds(r, 1)], sem).start(priority=kk % 2)
        return carry

    lax.fori_loop(0, tm, body, 0)
    for kk in range(TOP_K):
        pltpu.make_async_copy(ys_hbm.at[pl.ds(0, tm)], buf.at[kk], sem).wait()
    meta = meta_ref[...]
    y = meta[:, TOP_K:TOP_K + 1] * buf[0]
    for kk in range(1, TOP_K):
        y = y + meta[:, TOP_K + kk:TOP_K + kk + 1] * buf[kk]
    x = x_ref[...]
    gate = mod_ref[0][5:6]
    o_ref[...] = _layer_norm(alpha * x + (1.0 + gate) * y, png_ref[...], pnb_ref[...])


def _combine(x1, mod, seq, alpha, meta, dest, ys, png, pnb):
    t, d = x1.shape
    tm = DMA_ROW_TILE
    row = lambda i: (i, 0)
    full2 = lambda i: (0, 0)
    return pl.pallas_call(
        functools.partial(_combine_kernel, alpha=alpha),
        out_shape=jax.ShapeDtypeStruct((t, d), F32),
        grid=(t // tm,),
        in_specs=[
            pl.BlockSpec((1, 1, tm * TOP_K), lambda i: (i, 0, 0), memory_space=pltpu.SMEM),
            pl.BlockSpec((tm, d), row),
            pl.BlockSpec((1, 6, d), lambda i: ((i * tm) // seq, 0, 0)),
            pl.BlockSpec((tm, LANES), row),
            pl.BlockSpec(memory_space=pl.ANY),
            pl.BlockSpec((1, d), full2),
            pl.BlockSpec((1, d), full2),
        ],
        out_specs=pl.BlockSpec((tm, d), row),
        scratch_shapes=[pltpu.VMEM((TOP_K, tm, d), F32), pltpu.SemaphoreType.DMA(())],
        compiler_params=_cparams("arbitrary"),
        name="moe_combine",
    )(dest.reshape(t // tm, 1, tm * TOP_K), x1, mod, meta, ys, png.reshape(1, -1), pnb.reshape(1, -1))


def _moe_layer(x1, mod, seq, alpha, layer, w_router, b_router, wu, bu, wd, bd, png, pnb):
    t, d = x1.shape
    n_exp = w_router.shape[1]
    tm = MOE_ROW_TILE
    h, meta, cnt = _route(x1, mod, seq, w_router, b_router)

    counts = cnt[0, :n_exp].astype(jnp.int32)
    padded = (counts + tm - 1) // tm * tm
    pad_end = jnp.cumsum(padded)
    pad_start = pad_end - padded
    idx = meta[:, :TOP_K].astype(jnp.int32)
    rank = meta[:, 2 * TOP_K:3 * TOP_K].astype(jnp.int32)
    dest = pad_start[idx] + rank
    nblk = -(-(t * TOP_K + n_exp * tm) // tm)
    blk_start = jnp.arange(nblk, dtype=jnp.int32) * tm
    blk_e = jnp.minimum(jnp.sum(blk_start[:, None] >= pad_end[None, :], axis=1), n_exp - 1).astype(jnp.int32)
    nused = (pad_end[-1:] // tm).astype(jnp.int32)

    xs = _scatter_rows(h, dest, pad_end.astype(jnp.int32), nused, nblk * tm)
    ys = _expert_ffn(xs, blk_e + layer * n_exp, nused, wu, bu, wd, bd)
    return _combine(x1, mod, seq, alpha, meta, dest, ys, png, pnb)


def kernel(x, c, ada_w, ada_b, ln_g, ln_b, gla_w_in, gla_wg2_f, gla_bg_f, gla_wg2_b, gla_bg_b, gla_norm_g, gla_w_out, sgu_w_in, sgu_b_in, sgu_ln_g, sgu_ln_b, sgu_w_s, sgu_b_s, sgu_w_out, sgu_b_out, moe_w_router, moe_b_router, moe_w_up, moe_b_up, moe_w_down, moe_b_down):
    bsz, seq, d = x.shape
    depth = ada_w.shape[0]
    alpha = (2.0 * depth) ** 0.25
    mod_all = _adaln(c, ada_w, ada_b).reshape(depth, bsz, 6, d)
    xt = x.reshape(bsz * seq, d)
    n_exp = moe_w_up.shape[1]
    wu = _regroup_even_odd(moe_w_up.reshape((depth * n_exp,) + moe_w_up.shape[2:]))
    bu = jnp.concatenate([moe_b_up[..., 0::2], moe_b_up[..., 1::2]], axis=-1).reshape(depth * n_exp, 1, -1)
    wd = moe_w_down.reshape((depth * n_exp,) + moe_w_down.shape[2:]).astype(BF16)
    bd = moe_b_down.reshape(depth * n_exp, 1, -1)
    for layer in range(depth):
        mod = mod_all[layer]
        i = layer // 2
        if layer % 2 == 0:
            x1 = _gla_layer(xt, mod, bsz, seq, alpha, gla_w_in[i], gla_wg2_f[i], gla_bg_f[i],
                            gla_wg2_b[i], gla_bg_b[i], gla_norm_g[i], gla_w_out[i],
                            ln_g[layer, 0], ln_b[layer, 0])
        else:
            x1 = _sgu_layer(xt, mod, seq, alpha, sgu_w_in[i], sgu_b_in[i], sgu_ln_g[i], sgu_ln_b[i],
                            sgu_w_s[i], sgu_b_s[i], sgu_w_out[i], sgu_b_out[i],
                            ln_g[layer, 0], ln_b[layer, 0])
        xt = _moe_layer(x1, mod, seq, alpha, layer, moe_w_router[layer], moe_b_router[layer],
                        wu, bu, wd, bd, ln_g[layer, 1], ln_b[layer, 1])
    return xt.reshape(bsz, seq, d)
```
